```python
import math
import jax, jax.numpy as jnp
from jax import lax
import numpy as np

D_MODEL = 2048
BATCH = 16
SEQ = 2048
DEPTH = 4

N_META = 16
N_MIXERS = 2
MIX_WIDTH = D_MODEL
RMS_EPS = 1e-6
S5_GROUP = 16
S5_GROUPS = MIX_WIDTH // S5_GROUP
S5_STATE = 64
S5_CHUNK = 128
DT_MIN = 1e-3
DT_MAX = 1e-1
POOL_WINDOWS = (2, 4, 8, 16)
POOL_GROUPS = len(POOL_WINDOWS)
POOL_GROUP_WIDTH = MIX_WIDTH // POOL_GROUPS
N_EXPERT_GROUPS = 4
EXPERTS_PER_GROUP = 8
N_EXPERTS = N_EXPERT_GROUPS * EXPERTS_PER_GROUP
TOP_K_INNER = 2
D_EXPERT = D_MODEL // 4

kernel_name = 'hybrid_s5_pool_hmoe_trunk'

F32 = jnp.float32


def rms_norm(h, g):
    hf = h.astype(F32)
    y = hf * lax.rsqrt(jnp.mean(hf * hf, axis=-1, keepdims=True) + RMS_EPS) * g.astype(F32)
    return y.astype(h.dtype)


def s5_mixer(h, w_in, a_re, a_im, log_dt, b_re, b_im, c_re, c_im, d_skip, w_glu):
    bsz, t_len, _ = h.shape
    u = (h @ w_in).astype(F32).reshape(bsz, t_len, S5_GROUPS, S5_GROUP)
    dt = jnp.exp(log_dt.astype(F32))[:, None]
    lr, li = a_re.astype(F32), a_im.astype(F32)
    mag = jnp.exp(lr * dt)
    abar_re = mag * jnp.cos(li * dt)
    abar_im = mag * jnp.sin(li * dt)
    den = lr * lr + li * li
    nr, ni = abar_re - 1.0, abar_im
    coef_re = (nr * lr + ni * li) / den
    coef_im = (ni * lr - nr * li) / den
    br, bi = b_re.astype(F32), b_im.astype(F32)
    bbar_re = coef_re[..., None] * br - coef_im[..., None] * bi
    bbar_im = coef_re[..., None] * bi + coef_im[..., None] * br
    cr, ci = c_re.astype(F32), c_im.astype(F32)

    def combine(e1, e2):
        a1r, a1i, b1r, b1i = e1
        a2r, a2i, b2r, b2i = e2
        return (a1r * a2r - a1i * a2i, a1r * a2i + a1i * a2r,
                a2r * b1r - a2i * b1i + b2r, a2r * b1i + a2i * b1r + b2i)

    def chunk_scan(u_c, h0_re, h0_im):
        bu_re = jnp.einsum('blgh,gph->blgp', u_c, bbar_re)
        bu_im = jnp.einsum('blgh,gph->blgp', u_c, bbar_im)
        ar = jnp.broadcast_to(abar_re, bu_re.shape)
        ai = jnp.broadcast_to(abar_im, bu_re.shape)
        pr, pi, sr, si = lax.associative_scan(combine, (ar, ai, bu_re, bu_im), axis=1)
        xr = sr + pr * h0_re[:, None] - pi * h0_im[:, None]
        xi = si + pr * h0_im[:, None] + pi * h0_re[:, None]
        y = jnp.einsum('blgp,ghp->blgh', xr, cr) - jnp.einsum('blgp,ghp->blgh', xi, ci)
        return y, xr[:, -1], xi[:, -1]

    zeros = jnp.zeros((bsz, S5_GROUPS, S5_STATE), F32)
    y_meta, h_re, h_im = chunk_scan(u[:, :N_META], zeros, zeros)
    n_chunks = (t_len - N_META) // S5_CHUNK
    u_real = u[:, N_META:].reshape(bsz, n_chunks, S5_CHUNK, S5_GROUPS, S5_GROUP)
    u_real = jnp.moveaxis(u_real, 1, 0)

    def body(carry, u_c):
        y_c, hr, hi = chunk_scan(u_c, carry[0], carry[1])
        return (hr, hi), y_c

    _, y_chunks = lax.scan(body, (h_re, h_im), u_real)
    y_real = jnp.moveaxis(y_chunks, 0, 1).reshape(bsz, t_len - N_META, S5_GROUPS, S5_GROUP)
    y = jnp.concatenate([y_meta, y_real], axis=1)
    y = y + d_skip.astype(F32).reshape(S5_GROUPS, S5_GROUP) * u
    z = jax.nn.gelu(y.reshape(bsz, t_len, MIX_WIDTH))
    vg = z @ w_glu
    out = vg[..., :D_MODEL] * jax.nn.sigmoid(vg[..., D_MODEL:])
    return out.astype(h.dtype)


def pool_mixer(h, w_in, w_grp, scale):
    bsz, t_len, _ = h.shape
    u = (h @ w_in).astype(F32).reshape(bsz, t_len, POOL_GROUPS, POOL_GROUP_WIDTH)
    cs = jnp.cumsum(u, axis=1)
    t_idx = jnp.arange(t_len)
    pooled = []
    for g, w in enumerate(POOL_WINDOWS):
        csg = cs[:, :, g]
        lag = jnp.pad(csg, ((0, 0), (w, 0), (0, 0)))[:, :t_len]
        cnt = jnp.minimum(t_idx + 1, w).astype(F32)[None, :, None]
        pooled.append((csg - lag) / cnt)
    pooled = jnp.stack(pooled, axis=2)
    mixed = pooled - u
    y = jnp.einsum('btgc,gcd->btgd', mixed, w_grp.astype(F32)).reshape(bsz, t_len, D_MODEL)
    return (y * scale.astype(F32)).astype(h.dtype)


def hier_moe(h, w_coarse, b_coarse, w_fine, b_fine, w_gate, w_up, w_down):
    bsz, t_len, d = h.shape
    xt = h.reshape(-1, d)
    n_tok = xt.shape[0]
    coarse_p = jax.nn.softmax((xt @ w_coarse).astype(F32) + b_coarse.astype(F32), axis=-1)
    p_grp, grp = lax.top_k(coarse_p, 1)
    fine_logits = ((xt @ w_fine).astype(F32) + b_fine.astype(F32)).reshape(n_tok, N_EXPERT_GROUPS, EXPERTS_PER_GROUP)
    idx = jnp.broadcast_to(grp[:, :, None], (n_tok, 1, EXPERTS_PER_GROUP))
    fine_sel = jnp.take_along_axis(fine_logits, idx, axis=1)[:, 0]
    p_exp, exp_local = lax.top_k(jax.nn.softmax(fine_sel, axis=-1), TOP_K_INNER)
    w_exp = p_exp / jnp.sum(p_exp, axis=-1, keepdims=True) * p_grp
    exp_idx = grp * EXPERTS_PER_GROUP + exp_local
    gates = jnp.sum(jax.nn.one_hot(exp_idx, N_EXPERTS, dtype=F32) * w_exp[..., None], axis=1)
    out = jnp.zeros((n_tok, d), F32)
    for e in range(N_EXPERTS):
        hid = jax.nn.silu(xt @ w_gate[e]) * (xt @ w_up[e])
        out = out + gates[:, e:e + 1] * (hid @ w_down[e]).astype(F32)
    return out.astype(h.dtype).reshape(bsz, t_len, d)


def setup_inputs(seed: int = 0) -> dict:
    key = jax.random.key(seed)
    ks = iter(jax.random.split(key, 40))
    n_a = (DEPTH + N_MIXERS - 1) // N_MIXERS
    n_b = DEPTH // N_MIXERS

    def nrm(shape, scale):
        return jax.random.normal(next(ks), shape, F32) * scale

    ns = jnp.arange(S5_STATE, dtype=F32)
    return {
        'x': nrm((BATCH, SEQ, D_MODEL), 1.0),
        'meta_tokens': nrm((N_META, D_MODEL), 1.0),
        'norm_mix': 1.0 + nrm((DEPTH, D_MODEL), 0.02),
        'norm_ffn': 1.0 + nrm((DEPTH, D_MODEL), 0.02),
        'norm_final': 1.0 + nrm((D_MODEL,), 0.02),
        's5_w_in': nrm((n_a, D_MODEL, MIX_WIDTH), D_MODEL ** -0.5),
        's5_a_re': -0.5 + nrm((n_a, S5_GROUPS, S5_STATE), 0.01),
        's5_a_im': jnp.pi * ns + nrm((n_a, S5_GROUPS, S5_STATE), 0.01),
        's5_log_dt': jax.random.uniform(next(ks), (n_a, S5_GROUPS), F32, math.log(DT_MIN), math.log(DT_MAX)),
        's5_b_re': nrm((n_a, S5_GROUPS, S5_STATE, S5_GROUP), (2 * S5_GROUP) ** -0.5),
        's5_b_im': nrm((n_a, S5_GROUPS, S5_STATE, S5_GROUP), (2 * S5_GROUP) ** -0.5),
        's5_c_re': nrm((n_a, S5_GROUPS, S5_GROUP, S5_STATE), S5_STATE ** -0.5),
        's5_c_im': nrm((n_a, S5_GROUPS, S5_GROUP, S5_STATE), S5_STATE ** -0.5),
        's5_d': nrm((n_a, MIX_WIDTH), 1.0),
        's5_w_glu': nrm((n_a, MIX_WIDTH, 2 * D_MODEL), MIX_WIDTH ** -0.5),
        'pool_w_in': nrm((n_b, D_MODEL, MIX_WIDTH), D_MODEL ** -0.5),
        'pool_w_grp': nrm((n_b, POOL_GROUPS, POOL_GROUP_WIDTH, POOL_GROUP_WIDTH), POOL_GROUP_WIDTH ** -0.5),
        'pool_scale': 1.0 + nrm((n_b, D_MODEL), 0.02),
        'moe_w_coarse': nrm((DEPTH, D_MODEL, N_EXPERT_GROUPS), D_MODEL ** -0.5),
        'moe_b_coarse': nrm((DEPTH, N_EXPERT_GROUPS), 0.01),
        'moe_w_fine': nrm((DEPTH, D_MODEL, N_EXPERTS), D_MODEL ** -0.5),
        'moe_b_fine': nrm((DEPTH, N_EXPERTS), 0.01),
        'moe_w_gate': nrm((DEPTH, N_EXPERTS, D_MODEL, D_EXPERT), D_MODEL ** -0.5),
        'moe_w_up': nrm((DEPTH, N_EXPERTS, D_MODEL, D_EXPERT), D_MODEL ** -0.5),
        'moe_w_down': nrm((DEPTH, N_EXPERTS, D_EXPERT, D_MODEL), D_EXPERT ** -0.5),
    }


def reference(x, meta_tokens, norm_mix, norm_ffn, norm_final,
              s5_w_in, s5_a_re, s5_a_im, s5_log_dt, s5_b_re, s5_b_im, s5_c_re, s5_c_im, s5_d, s5_w_glu,
              pool_w_in, pool_w_grp, pool_scale,
              moe_w_coarse, moe_b_coarse, moe_w_fine, moe_b_fine, moe_w_gate, moe_w_up, moe_w_down):
    bsz = x.shape[0]
    meta = jnp.broadcast_to(meta_tokens[None].astype(x.dtype), (bsz, N_META, D_MODEL))
    h = jnp.concatenate([meta, x], axis=1)
    for i in range(DEPTH):
        hn = rms_norm(h, norm_mix[i])
        j = i // N_MIXERS
        if i % N_MIXERS == 0:
            h = h + s5_mixer(hn, s5_w_in[j], s5_a_re[j], s5_a_im[j], s5_log_dt[j], s5_b_re[j], s5_b_im[j],
                             s5_c_re[j], s5_c_im[j], s5_d[j], s5_w_glu[j])
        else:
            h = h + pool_mixer(hn, pool_w_in[j], pool_w_grp[j], pool_scale[j])
        h = h + hier_moe(rms_norm(h, norm_ffn[i]), moe_w_coarse[i], moe_b_coarse[i], moe_w_fine[i],
                         moe_b_fine[i], moe_w_gate[i], moe_w_up[i], moe_w_down[i])
    h = rms_norm(h, norm_final)
    return h[:, N_META:]
```

```python
import functools

import jax
import jax.numpy as jnp
from jax import lax
from jax.experimental import pallas as pl
from jax.experimental.pallas import tpu as pltpu

F32 = jnp.float32
BF16 = jnp.bfloat16
I32 = jnp.int32

N_META = 16
CHUNK = 16
RMS_EPS = 1e-6
S5_GROUP = 16
S5_STATE = 64
CHUNK_W = CHUNK * S5_GROUP
POOL_WINDOWS = (2, 4, 8, 16)
N_EXPERT_GROUPS = 4
EXPERTS_PER_GROUP = 8
N_EXPERTS = N_EXPERT_GROUPS * EXPERTS_PER_GROUP
ROUTER_ROWS = 128
FINE_ROW0 = 8

LANE = 128
TOK_TILE = 256
EXP_TILE = 256
VMEM_LIMIT = 56 * 1024 * 1024

_NT = (((1,), (1,)), ((), ()))
_TN = (((0,), (0,)), ((), ()))


def _cparams(*sem):
    return pltpu.CompilerParams(dimension_semantics=sem, vmem_limit_bytes=VMEM_LIMIT)


def _rms(x, g):
    return x * lax.rsqrt(jnp.mean(x * x, axis=-1, keepdims=True) + RMS_EPS) * g


def _pack_pairs(lo, hi):
    return pltpu.pack_elementwise([lo, hi], packed_dtype=BF16)


def _unpack_pairs(w):
    lo = pltpu.unpack_elementwise(w, index=0, packed_dtype=BF16, unpacked_dtype=F32)
    hi = pltpu.unpack_elementwise(w, index=1, packed_dtype=BF16, unpacked_dtype=F32)
    return lo, hi


def _rmsnorm_kernel(h_ref, g_ref, o_ref):
    o_ref[...] = _rms(h_ref[...], g_ref[...]).astype(o_ref.dtype)


def _rmsnorm(h, g, out_dtype):
    n, d = h.shape
    return pl.pallas_call(
        _rmsnorm_kernel,
        grid=(n // TOK_TILE,),
        in_specs=[pl.BlockSpec((TOK_TILE, d), lambda i: (i, 0)),
                  pl.BlockSpec((1, d), lambda i: (0, 0))],
        out_specs=pl.BlockSpec((TOK_TILE, d), lambda i: (i, 0)),
        out_shape=jax.ShapeDtypeStruct((n, d), out_dtype),
        compiler_params=_cparams("parallel"),
        name="rmsnorm",
    )(h, g.reshape(1, d))


def _mm_kernel(x_ref, w_ref, o_ref):
    o_ref[...] = jnp.dot(x_ref[...], w_ref[...], preferred_element_type=F32).astype(o_ref.dtype)


def _matmul(x, w, out_dtype, tm=512):
    n, k = x.shape
    f = w.shape[1]
    return pl.pallas_call(
        _mm_kernel,
        grid=(n // tm,),
        in_specs=[pl.BlockSpec((tm, k), lambda i: (i, 0)),
                  pl.BlockSpec((k, f), lambda i: (0, 0))],
        out_specs=pl.BlockSpec((tm, f), lambda i: (i, 0)),
        out_shape=jax.ShapeDtypeStruct((n, f), out_dtype),
        compiler_params=_cparams("parallel"),
        name="pool_in",
    )(x, w)


def _mm_nt_kernel(w_ref, x_ref, o_ref):
    o_ref[...] = lax.dot_general(w_ref[...], x_ref[...], _NT,
                                 preferred_element_type=F32).astype(o_ref.dtype)


def _s5_in(hn, wt, slp):
    n, k = hn.shape
    f = wt.shape[0]
    tf = min(512, f)
    return pl.pallas_call(
        _mm_nt_kernel,
        grid=(CHUNK, f // tf),
        in_specs=[pl.BlockSpec((tf, k), lambda t, j: (j, 0)),
                  pl.BlockSpec((None, slp, k), lambda t, j: (t, 0, 0))],
        out_specs=pl.BlockSpec((None, tf, slp), lambda t, j: (t, j, 0)),
        out_shape=jax.ShapeDtypeStruct((CHUNK, f, slp), BF16),
        compiler_params=_cparams("parallel", "arbitrary"),
        name="s5_in",
    )(wt, hn.reshape(CHUNK, slp, k))


def _s5_prep_kernel(lrc_ref, lic_ref, ldt_ref, brt_ref, bit_ref, cre_ref, cim_ref,
                    lrr_ref, lir_ref, crt_ref, cit_ref,
                    m_ref, wz_ref, wyre_ref, wyim_ref, a16re_ref, a16im_ref):
    g = pl.program_id(0)
    p = S5_STATE
    dt = jnp.exp(ldt_ref[0:1, :])
    lr = lrc_ref[...]
    li = lic_ref[...]
    lam_r = lr * dt
    lam_i = li * dt
    mag = jnp.exp(lam_r)
    abr = mag * jnp.cos(lam_i)
    abi = mag * jnp.sin(lam_i)
    den = lr * lr + li * li
    nr = abr - 1.0
    cfr = (nr * lr + abi * li) / den
    cfi = (abi * lr - nr * li) / den

    def two(x):
        return jnp.concatenate([x, x], axis=1)

    cfr2, cfi2, lamr2, lami2 = two(cfr), two(cfi), two(lam_r), two(lam_i)
    brt = brt_ref[...]
    bit = bit_ref[...]
    bbr = cfr2 * brt - cfi2 * bit
    bbi = cfr2 * bit + cfi2 * brt
    lane = lax.broadcasted_iota(I32, (1, CHUNK_W), 1)
    k = (CHUNK - 1 - (lane >> 4)).astype(F32)
    er = jnp.exp(k * lamr2)
    pr = er * jnp.cos(k * lami2)
    pi = er * jnp.sin(k * lami2)
    wzr = pr * bbr - pi * bbi
    wzi = pr * bbi + pi * bbr
    wz_ref[0:p, :] = wzr.astype(BF16)
    wz_ref[p:2 * p, :] = wzi.astype(BF16)

    krev = (jnp.dot(cre_ref[...], wzr, precision=lax.Precision.HIGHEST, preferred_element_type=F32)
            - jnp.dot(cim_ref[...], wzi, precision=lax.Precision.HIGHEST, preferred_element_type=F32))
    for t in range(CHUNK):
        s = (CHUNK - 1 - t) * S5_GROUP
        blk = krev if s == 0 else pltpu.roll(krev, CHUNK_W - s, 1)
        blk = jnp.where(lane < CHUNK_W - s, blk, 0.0)
        m_ref[t * S5_GROUP:(t + 1) * S5_GROUP, :] = blk.astype(BF16)

    row = lax.broadcasted_iota(I32, (CHUNK_W, 1), 0)
    kk = ((row >> 4) + 1).astype(F32)
    lamr_row = lrr_ref[0:1, :] * dt
    lami_row = lir_ref[0:1, :] * dt
    erow = jnp.exp(kk * lamr_row)
    pwr = erow * jnp.cos(kk * lami_row)
    pwi = erow * jnp.sin(kk * lami_row)
    cr2 = crt_ref[...]
    ci2 = cit_ref[...]
    lane128 = lax.broadcasted_iota(I32, (1, LANE), 1)
    sel = (lane128 >> 6) == (g % 2)
    wyre_ref[...] = jnp.where(sel, cr2 * pwr - ci2 * pwi, 0.0).astype(BF16)
    wyim_ref[...] = jnp.where(sel, -(cr2 * pwi) - ci2 * pwr, 0.0).astype(BF16)

    e16 = jnp.exp(float(CHUNK) * lamr_row)
    a16re_ref[...] = jnp.broadcast_to(e16 * jnp.cos(float(CHUNK) * lami_row), (8, LANE))
    a16im_ref[...] = jnp.broadcast_to(e16 * jnp.sin(float(CHUNK) * lami_row), (8, LANE))


def _s5_prep(a_re, a_im, log_dt, b_re, b_im, c_re, c_im):
    g, p = a_re.shape
    lrc = jnp.broadcast_to(a_re[:, :, None], (g, p, LANE))
    lic = jnp.broadcast_to(a_im[:, :, None], (g, p, LANE))
    ldt = jnp.broadcast_to(log_dt[:, None, None], (g, 8, LANE))
    brt = jnp.tile(b_re, (1, 1, CHUNK))
    bit = jnp.tile(b_im, (1, 1, CHUNK))
    lrr = jnp.broadcast_to(jnp.concatenate([a_re, a_re], axis=1)[:, None, :], (g, 8, LANE))
    lir = jnp.broadcast_to(jnp.concatenate([a_im, a_im], axis=1)[:, None, :], (g, 8, LANE))
    crt = jnp.tile(jnp.concatenate([c_re, c_re], axis=2), (1, CHUNK, 1))
    cit = jnp.tile(jnp.concatenate([c_im, c_im], axis=2), (1, CHUNK, 1))

    def spec(r, c):
        return pl.BlockSpec((None, r, c), lambda i: (i, 0, 0))

    return pl.pallas_call(
        _s5_prep_kernel,
        grid=(g,),
        in_specs=[spec(p, LANE), spec(p, LANE), spec(8, LANE), spec(p, CHUNK_W), spec(p, CHUNK_W),
                  spec(S5_GROUP, p), spec(S5_GROUP, p), spec(8, LANE), spec(8, LANE),
                  spec(CHUNK_W, LANE), spec(CHUNK_W, LANE)],
        out_specs=[spec(CHUNK_W, CHUNK_W), spec(2 * p, CHUNK_W), spec(CHUNK_W, LANE),
                   spec(CHUNK_W, LANE), spec(8, LANE), spec(8, LANE)],
        out_shape=[jax.ShapeDtypeStruct((g, CHUNK_W, CHUNK_W), BF16),
                   jax.ShapeDtypeStruct((g, 2 * p, CHUNK_W), BF16),
                   jax.ShapeDtypeStruct((g, CHUNK_W, LANE), BF16),
                   jax.ShapeDtypeStruct((g, CHUNK_W, LANE), BF16),
                   jax.ShapeDtypeStruct((g, 8, LANE), F32),
                   jax.ShapeDtypeStruct((g, 8, LANE), F32)],
        compiler_params=_cparams("parallel"),
        name="s5_prep",
    )(lrc, lic, ldt, brt, bit, c_re, c_im, lrr, lir, crt, cit)


def _gelu_tanh(x):
    return x * (0.5 * (1.0 + jnp.tanh(0.7978845608028654 * (x + 0.044715 * (x * x * x)))))


def _s5_core_kernel(ut_ref, m_ref, wz_ref, wyre_ref, wyim_ref, are_ref, aim_ref, d_ref,
                    o_ref, zre_s, zim_s, spre_s, spim_s, *, nb, ncp):
    p = S5_STATE
    slp = nb * ncp
    vs, zs = [], []
    for q in range(2):
        v = ut_ref[:, q * S5_GROUP:(q + 1) * S5_GROUP, :].reshape(CHUNK_W, slp)
        vs.append(v)
        zs.append(jnp.dot(wz_ref[q], v, preferred_element_type=F32))
    zre_s[...] = jnp.concatenate([zs[0][0:p], zs[1][0:p]], axis=0).T
    zim_s[...] = jnp.concatenate([zs[0][p:2 * p], zs[1][p:2 * p]], axis=0).T

    lane = lax.broadcasted_iota(I32, (1, LANE), 1)
    are = jnp.where(lane < p, are_ref[0, 0:1, :], are_ref[1, 0:1, :])
    aim = jnp.where(lane < p, aim_ref[0, 0:1, :], aim_ref[1, 0:1, :])

    def step(c, carry):
        sr, si = carry
        r0 = pl.multiple_of(c * nb, nb)
        spre_s[pl.ds(r0, nb), :] = sr.astype(BF16)
        spim_s[pl.ds(r0, nb), :] = si.astype(BF16)
        zr = zre_s[pl.ds(r0, nb), :]
        zi = zim_s[pl.ds(r0, nb), :]
        return are * sr - aim * si + zr, are * si + aim * sr + zi

    zero = jnp.zeros((nb, LANE), F32)
    lax.fori_loop(0, ncp, step, (zero, zero))

    spre = spre_s[...]
    spim = spim_s[...]
    for q in range(2):
        y_in = jnp.dot(m_ref[q], vs[q], preferred_element_type=F32)
        y_st = (lax.dot_general(wyre_ref[q], spre, _NT, preferred_element_type=F32)
                + lax.dot_general(wyim_ref[q], spim, _NT, preferred_element_type=F32))
        d = d_ref[q]
        for jb in range(slp // LANE):
            sl = slice(jb * LANE, (jb + 1) * LANE)
            y = y_in[:, sl] + y_st[:, sl] + d * vs[q][:, sl].astype(F32)
            z = _gelu_tanh(y).astype(BF16)
            o_ref[:, q * S5_GROUP:(q + 1) * S5_GROUP, sl] = z.reshape(CHUNK, S5_GROUP, LANE)


def _s5_core(ut, ops, d_b, nb, ncp):
    m, wz, wyre, wyim, a16re, a16im = ops
    _, f, slp = ut.shape
    g = f // S5_GROUP

    def pair(r, c):
        return pl.BlockSpec((2, r, c), lambda i: (i, 0, 0))

    io_spec = pl.BlockSpec((CHUNK, 2 * S5_GROUP, slp), lambda i: (0, i, 0))
    return pl.pallas_call(
        functools.partial(_s5_core_kernel, nb=nb, ncp=ncp),
        grid=(g // 2,),
        in_specs=[io_spec, pair(CHUNK_W, CHUNK_W), pair(2 * S5_STATE, CHUNK_W),
                  pair(CHUNK_W, LANE), pair(CHUNK_W, LANE), pair(8, LANE), pair(8, LANE),
                  pair(CHUNK_W, LANE)],
        out_specs=io_spec,
        out_shape=jax.ShapeDtypeStruct(ut.shape, BF16),
        scratch_shapes=[pltpu.VMEM((slp, LANE), F32), pltpu.VMEM((slp, LANE), F32),
                        pltpu.VMEM((slp, LANE), BF16), pltpu.VMEM((slp, LANE), BF16)],
        compiler_params=_cparams("parallel"),
        name="s5_core",
    )(ut, m, wz, wyre, wyim, a16re, a16im, d_b)


def _s5_out_kernel(z_ref, wv_ref, wg_ref, h_ref, o_ref):
    zt = z_ref[...]
    v = lax.dot_general(zt, wv_ref[...], _TN, preferred_element_type=F32)
    gate = lax.dot_general(zt, wg_ref[...], _TN, preferred_element_type=F32)
    o_ref[...] = h_ref[...] + v * jax.nn.sigmoid(gate)


def _s5_out(zt, w_glu, h, slp):
    _, f, _ = zt.shape
    n, d = h.shape
    tn = min(1024, d)
    nj = d // tn
    h3 = h.reshape(CHUNK, slp, d)
    out = pl.pallas_call(
        _s5_out_kernel,
        grid=(nj, CHUNK, slp // LANE),
        in_specs=[pl.BlockSpec((None, f, LANE), lambda j, t, i: (t, 0, i)),
                  pl.BlockSpec((f, tn), lambda j, t, i: (0, j)),
                  pl.BlockSpec((f, tn), lambda j, t, i: (0, nj + j)),
                  pl.BlockSpec((None, LANE, tn), lambda j, t, i: (t, i, j))],
        out_specs=pl.BlockSpec((None, LANE, tn), lambda j, t, i: (t, i, j)),
        out_shape=jax.ShapeDtypeStruct(h3.shape, F32),
        compiler_params=_cparams("arbitrary", "arbitrary", "arbitrary"),
        name="s5_out",
    )(zt, w_glu, w_glu, h3)
    return out.reshape(n, d)


def _pool_body(win, first, u_ref, halo_ref, w_ref, sc_ref, h_ref, o_ref, nb):
    rows = u_ref.shape[1]
    row = lax.broadcasted_iota(I32, (rows, 1), 0)

    def cur(t):
        return u_ref[t]

    def prev(t):
        halo = jnp.where(first, 0.0, halo_ref[t])
        return jnp.concatenate([halo, u_ref[t, 0:rows - nb, :]], axis=0)

    def at(tau):
        return cur(tau) if tau >= 0 else prev(tau + CHUNK)

    run = at(0)
    for j in range(1, win):
        run = run + at(-j)
    wmat = w_ref[...]
    scale = sc_ref[...]
    for t in range(CHUNK):
        if t > 0:
            run = run + at(t) - at(t - win)
        if t + 1 >= win:
            inv = 1.0 / win
        else:
            inv = jnp.where(jnp.logical_and(first, row < nb), 1.0 / (t + 1), 1.0 / win)
        mixed = run * inv - cur(t)
        y = jnp.dot(mixed.astype(BF16), wmat, preferred_element_type=F32)
        o_ref[t] = h_ref[t] + y * scale


def _pool_kernel(u_ref, halo_ref, w_ref, sc_ref, h_ref, o_ref, *, nb):
    g = pl.program_id(0)
    first = pl.program_id(1) == 0
    for gi, win in enumerate(POOL_WINDOWS):
        @pl.when(g == gi)
        def _(win=win):
            _pool_body(win, first, u_ref, halo_ref, w_ref, sc_ref, h_ref, o_ref, nb)


def _pool_core(u, w_grp, scale, h, nb, slp):
    n, d = h.shape
    ng = len(POOL_WINDOWS)
    cw = d // ng
    rt = LANE
    per = rt // nb
    u3 = u.reshape(CHUNK, slp, d)
    h3 = h.reshape(CHUNK, slp, d)
    blk = pl.BlockSpec((CHUNK, rt, cw), lambda g, i: (0, i, g))
    out = pl.pallas_call(
        functools.partial(_pool_kernel, nb=nb),
        grid=(ng, slp // rt),
        in_specs=[blk,
                  pl.BlockSpec((CHUNK, nb, cw), lambda g, i: (0, jnp.maximum(i * per - 1, 0), g)),
                  pl.BlockSpec((None, cw, cw), lambda g, i: (g, 0, 0)),
                  pl.BlockSpec((1, cw), lambda g, i: (0, g)),
                  blk],
        out_specs=blk,
        out_shape=jax.ShapeDtypeStruct(h3.shape, F32),
        compiler_params=_cparams("parallel", "parallel"),
        name="pool_core",
    )(u3, u3, w_grp, scale.reshape(1, d), h3)
    return out.reshape(n, d)


def _router_kernel(h_ref, g_ref, wr_ref, b_ref, eid_ref, wts_ref, rank_ref, cnt_ref, carry_s):
    i = pl.program_id(0)
    tm = h_ref.shape[0]

    @pl.when(i == 0)
    def _():
        carry_s[...] = jnp.zeros_like(carry_s)

    xn = _rms(h_ref[...], g_ref[...])
    logits = lax.dot_general(wr_ref[...], xn, _NT, precision=lax.Precision.HIGHEST,
                             preferred_element_type=F32) + b_ref[:, 0:1]

    c = [logits[r:r + 1] for r in range(N_EXPERT_GROUPS)]
    cmax = jnp.maximum(jnp.maximum(c[0], c[1]), jnp.maximum(c[2], c[3]))
    grp = jnp.where(c[0] == cmax, 0, jnp.where(c[1] == cmax, 1, jnp.where(c[2] == cmax, 2, 3)))
    csum = (jnp.exp(c[0] - cmax) + jnp.exp(c[1] - cmax)) + (jnp.exp(c[2] - cmax) + jnp.exp(c[3] - cmax))
    p_grp = 1.0 / csum

    epg = EXPERTS_PER_GROUP
    sel = jnp.zeros((epg, tm), F32)
    for gi in range(N_EXPERT_GROUPS):
        blk = logits[FINE_ROW0 + gi * epg:FINE_ROW0 + (gi + 1) * epg]
        sel = jnp.where(grp == gi, blk, sel)
    ridx = lax.broadcasted_iota(I32, (epg, tm), 0)
    m1 = jnp.max(sel, axis=0, keepdims=True)
    i1 = jnp.min(jnp.where(sel == m1, ridx, epg), axis=0, keepdims=True)
    sel2 = jnp.where(ridx == i1, -jnp.inf, sel)
    m2 = jnp.max(sel2, axis=0, keepdims=True)
    i2 = jnp.min(jnp.where(sel2 == m2, ridx, epg), axis=0, keepdims=True)
    ssum = jnp.sum(jnp.exp(sel - m1), axis=0, keepdims=True)
    p1 = 1.0 / ssum
    p2 = jnp.exp(m2 - m1) / ssum
    den = p1 + p2
    e1 = grp * epg + i1
    e2 = grp * epg + i2
    eid_ref[0:1, :] = e1
    eid_ref[1:2, :] = e2
    wts_ref[0:1, :] = p1 / den * p_grp
    wts_ref[1:2, :] = p2 / den * p_grp

    eidx = lax.broadcasted_iota(I32, (N_EXPERTS, tm), 0)
    hit1 = eidx == e1
    hit2 = eidx == e2
    onehot = jnp.where(hit1, 1.0, jnp.where(hit2, 1.0, 0.0))
    tri = jnp.where(lax.broadcasted_iota(I32, (tm, tm), 0) <= lax.broadcasted_iota(I32, (tm, tm), 1),
                    1.0, 0.0).astype(BF16)
    pref = jnp.dot(onehot.astype(BF16), tri, preferred_element_type=F32)
    tot = pref + carry_s[:, 0:1]
    rank_ref[0:1, :] = (jnp.sum(jnp.where(hit1, tot, 0.0), axis=0, keepdims=True) - 1.0).astype(I32)
    rank_ref[1:2, :] = (jnp.sum(jnp.where(hit2, tot, 0.0), axis=0, keepdims=True) - 1.0).astype(I32)
    carry_s[...] = carry_s[...] + pref[:, tm - 1:tm]
    cnt_ref[...] = carry_s[...]


def _router(h, g, wr_t, bias):
    n, d = h.shape
    tm = TOK_TILE
    two = pl.BlockSpec((2, tm), lambda i: (0, i))
    return pl.pallas_call(
        _router_kernel,
        grid=(n // tm,),
        in_specs=[pl.BlockSpec((tm, d), lambda i: (i, 0)),
                  pl.BlockSpec((1, d), lambda i: (0, 0)),
                  pl.BlockSpec((ROUTER_ROWS, d), lambda i: (0, 0)),
                  pl.BlockSpec((ROUTER_ROWS, LANE), lambda i: (0, 0))],
        out_specs=[two, two, two, pl.BlockSpec((N_EXPERTS, LANE), lambda i: (0, 0))],
        out_shape=[jax.ShapeDtypeStruct((2, n), I32), jax.ShapeDtypeStruct((2, n), F32),
                   jax.ShapeDtypeStruct((2, n), I32), jax.ShapeDtypeStruct((N_EXPERTS, LANE), F32)],
        scratch_shapes=[pltpu.VMEM((N_EXPERTS, LANE), F32)],
        compiler_params=_cparams("arbitrary"),
        name="moe_router",
    )(h, g.reshape(1, d), wr_t, bias)


def _router_params(w_coarse, b_coarse, w_fine, b_fine):
    d = w_coarse.shape[0]
    wr_t = jnp.zeros((ROUTER_ROWS, d), F32)
    wr_t = wr_t.at[0:N_EXPERT_GROUPS].set(w_coarse.T)
    wr_t = wr_t.at[FINE_ROW0:FINE_ROW0 + N_EXPERTS].set(w_fine.T)
    bias = jnp.zeros((ROUTER_ROWS,), F32)
    bias = bias.at[0:N_EXPERT_GROUPS].set(b_coarse)
    bias = bias.at[FINE_ROW0:FINE_ROW0 + N_EXPERTS].set(b_fine)
    return wr_t, jnp.broadcast_to(bias[:, None], (ROUTER_ROWS, LANE))


def _route_tables(eid, rank, counts, n_tiles):
    cnt = counts[:, 0].astype(I32)
    tiles_e = (cnt + EXP_TILE - 1) // EXP_TILE
    tile_end = jnp.cumsum(tiles_e)
    tile_start = tile_end - tiles_e
    n_used = tile_end[-1]
    tid = jnp.arange(n_tiles, dtype=I32)
    te = jnp.searchsorted(tile_end, jnp.minimum(tid, n_used - 1), side="right").astype(I32)
    te = jnp.minimum(te, N_EXPERTS - 1)
    valid = jnp.where(tid < n_used, jnp.clip(cnt[te] - (tid - tile_start[te]) * EXP_TILE, 0, EXP_TILE), 0)
    pos = tile_start[eid] * EXP_TILE + rank
    return pos.astype(I32), te, valid.astype(I32), jnp.reshape(n_used, (1,)).astype(I32)


def _dispatch_kernel(pos_ref, h_ref, g_ref, xs_in, xs_ref, buf, sem):
    del xs_in
    i = pl.program_id(0)
    n = pl.num_programs(0)
    tm = h_ref.shape[0]
    half = h_ref.shape[1] // 2
    slot = i % 2

    def wait_slot(s):
        for _ in range(2):
            pltpu.make_async_copy(buf.at[s], xs_ref.at[pl.ds(0, tm)], sem.at[s]).wait()

    @pl.when(i >= 2)
    def _():
        wait_slot(slot)

    xn = _rms(h_ref[...], g_ref[...])
    buf[slot] = _pack_pairs(xn[:, :half], xn[:, half:])

    def issue(r, carry):
        for k in range(2):
            pltpu.make_async_copy(buf.at[slot, pl.ds(r, 1)], xs_ref.at[pl.ds(pos_ref[k, r], 1)],
                                  sem.at[slot]).start()
        return carry

    lax.fori_loop(0, tm, issue, 0, unroll=8)

    @pl.when(i == n - 1)
    def _():
        wait_slot(slot)

        @pl.when(n >= 2)
        def _():
            wait_slot(1 - slot)


def _dispatch(h, g, pos, n_rows, word_dtype):
    n, d = h.shape
    tm = TOK_TILE
    pos3 = pos.reshape(2, n // tm, tm).transpose(1, 0, 2)
    xs0 = jnp.zeros((n_rows, d // 2), word_dtype)
    return pl.pallas_call(
        _dispatch_kernel,
        grid=(n // tm,),
        in_specs=[pl.BlockSpec((None, 2, tm), lambda i: (i, 0, 0), memory_space=pltpu.SMEM),
                  pl.BlockSpec((tm, d), lambda i: (i, 0)),
                  pl.BlockSpec((1, d), lambda i: (0, 0)),
                  pl.BlockSpec(memory_space=pl.ANY)],
        out_specs=pl.BlockSpec(memory_space=pl.ANY),
        out_shape=jax.ShapeDtypeStruct((n_rows, d // 2), word_dtype),
        scratch_shapes=[pltpu.VMEM((2, tm, d // 2), word_dtype), pltpu.SemaphoreType.DMA((2,))],
        input_output_aliases={3: 0},
        compiler_params=_cparams("arbitrary"),
        name="moe_dispatch",
    )(pos3, h, g.reshape(1, d), xs0)


def _expert_kernel(te_ref, va_ref, nu_ref, xs_ref, wg_ref, wu_ref, wd_ref, ys_ref, wgb, wub, wdb):
    del nu_ref
    i = pl.program_id(0)
    changed = jnp.logical_or(i == 0, te_ref[i] != te_ref[jnp.maximum(i - 1, 0)])

    @pl.when(changed)
    def _():
        wgb[...] = wg_ref[...].astype(BF16)
        wub[...] = wu_ref[...].astype(BF16)
        wdb[...] = wd_ref[...].astype(BF16)

    valid = va_ref[i]

    @pl.when(valid > 0)
    def _():
        lo, hi = _unpack_pairs(xs_ref[...])
        lo = lo.astype(BF16)
        hi = hi.astype(BF16)
        half = lo.shape[1]
        gate = (jnp.dot(lo, wgb[0:half, :], preferred_element_type=F32)
                + jnp.dot(hi, wgb[half:2 * half, :], preferred_element_type=F32))
        up = (jnp.dot(lo, wub[0:half, :], preferred_element_type=F32)
              + jnp.dot(hi, wub[half:2 * half, :], preferred_element_type=F32))
        hid = (gate * jax.nn.sigmoid(gate)) * up
        y = jnp.dot(hid.astype(BF16), wdb[...], preferred_element_type=F32)
        ys_ref[...] = _pack_pairs(y[:, :half], y[:, half:])

    @pl.when(valid <= 0)
    def _():
        ys_ref[...] = jnp.zeros_like(ys_ref)


def _experts(xs, te, valid, n_used, w_gate, w_up, w_down):
    n_rows, half = xs.shape
    _, d, de = w_gate.shape
    te_tile = EXP_TILE
    grid_spec = pltpu.PrefetchScalarGridSpec(
        num_scalar_prefetch=3,
        grid=(n_rows // te_tile,),
        in_specs=[pl.BlockSpec((te_tile, half), lambda i, te, va, nu: (jnp.minimum(i, nu[0] - 1), 0)),
                  pl.BlockSpec((None, d, de), lambda i, te, va, nu: (te[i], 0, 0)),
                  pl.BlockSpec((None, d, de), lambda i, te, va, nu: (te[i], 0, 0)),
                  pl.BlockSpec((None, de, d), lambda i, te, va, nu: (te[i], 0, 0))],
        out_specs=pl.BlockSpec((te_tile, half), lambda i, te, va, nu: (i, 0)),
        scratch_shapes=[pltpu.VMEM((d, de), BF16), pltpu.VMEM((d, de), BF16), pltpu.VMEM((de, d), BF16)],
    )
    return pl.pallas_call(
        _expert_kernel,
        grid_spec=grid_spec,
        out_shape=jax.ShapeDtypeStruct((n_rows, half), xs.dtype),
        compiler_params=_cparams("arbitrary"),
        name="moe_experts",
    )(te, valid, n_used, xs, w_gate, w_up, w_down)


def _combine_kernel(pos_ref, posn_ref, w_ref, h_ref, g_ref, ys_ref, o_ref, hn_ref, buf, sem):
    i = pl.program_id(0)
    n = pl.num_programs(0)
    tm = h_ref.shape[0]
    half = h_ref.shape[1] // 2
    slot = i % 2

    def issue(p_ref, s):
        def body(r, carry):
            for k in range(2):
                pltpu.make_async_copy(ys_ref.at[pl.ds(p_ref[k, r], 1)], buf.at[s, k, pl.ds(r, 1)],
                                      sem.at[s]).start()
            return carry
        lax.fori_loop(0, tm, body, 0, unroll=8)

    @pl.when(i == 0)
    def _():
        issue(pos_ref, 0)

    @pl.when(i + 1 < n)
    def _():
        issue(posn_ref, 1 - slot)

    for k in range(2):
        pltpu.make_async_copy(ys_ref.at[pl.ds(0, tm)], buf.at[slot, k], sem.at[slot]).wait()

    lo0, hi0 = _unpack_pairs(buf[slot, 0])
    lo1, hi1 = _unpack_pairs(buf[slot, 1])
    w0 = jnp.broadcast_to(w_ref[0:1, :], (LANE, tm)).T[:, 0:1]
    w1 = jnp.broadcast_to(w_ref[1:2, :], (LANE, tm)).T[:, 0:1]
    h = h_ref[...]
    hnew = jnp.concatenate([h[:, :half] + (w0 * lo0 + w1 * lo1),
                            h[:, half:] + (w0 * hi0 + w1 * hi1)], axis=1)
    o_ref[...] = hnew
    hn_ref[...] = _rms(hnew, g_ref[...]).astype(hn_ref.dtype)


def _combine(h, ys, pos, wts, g_next, hn_dtype):
    n, d = h.shape
    tm = TOK_TILE
    nt = n // tm
    pos3 = pos.reshape(2, nt, tm).transpose(1, 0, 2)
    smem = functools.partial(pl.BlockSpec, (None, 2, tm), memory_space=pltpu.SMEM)
    row = pl.BlockSpec((tm, d), lambda i: (i, 0))
    return pl.pallas_call(
        _combine_kernel,
        grid=(nt,),
        in_specs=[smem(lambda i: (i, 0, 0)),
                  smem(lambda i: (jnp.minimum(i + 1, nt - 1), 0, 0)),
                  pl.BlockSpec((2, tm), lambda i: (0, i)),
                  row,
                  pl.BlockSpec((1, d), lambda i: (0, 0)),
                  pl.BlockSpec(memory_space=pl.ANY)],
        out_specs=[row, row],
        out_shape=[jax.ShapeDtypeStruct((n, d), F32), jax.ShapeDtypeStruct((n, d), hn_dtype)],
        scratch_shapes=[pltpu.VMEM((2, 2, tm, d // 2), ys.dtype), pltpu.SemaphoreType.DMA((2,))],
        compiler_params=_cparams("arbitrary"),
        name="moe_combine",
    )(pos3, pos3, wts, h, g_next.reshape(1, d), ys)


def _hier_moe(h, g_ffn, g_next, hn_dtype, w_coarse, b_coarse, w_fine, b_fine, w_gate, w_up, w_down):
    n = h.shape[0]
    n_tiles = (2 * n + N_EXPERTS * (EXP_TILE - 1) + EXP_TILE - 1) // EXP_TILE
    wr_t, bias = _router_params(w_coarse, b_coarse, w_fine, b_fine)
    eid, wts, rank, counts = _router(h, g_ffn, wr_t, bias)
    pos, te, valid, n_used = _route_tables(eid, rank, counts, n_tiles)
    word_dtype = jax.eval_shape(lambda a: _pack_pairs(a, a), jax.ShapeDtypeStruct((8, LANE), F32)).dtype
    xs = _dispatch(h, g_ffn, pos, n_tiles * EXP_TILE, word_dtype)
    ys = _experts(xs, te, valid, n_used, w_gate, w_up, w_down)
    return _combine(h, ys, pos, wts, g_next, hn_dtype)


def _to_token_major(x, meta, ncp):
    b, seq, d = x.shape
    nc = seq // CHUNK
    xr = x.reshape(b, nc, CHUNK, d).transpose(2, 1, 0, 3)
    m = jnp.broadcast_to(meta[:, None, None, :].astype(x.dtype), (CHUNK, 1, b, d))
    pad = jnp.zeros((CHUNK, ncp - nc - 1, b, d), x.dtype)
    return jnp.concatenate([m, xr, pad], axis=1).reshape(CHUNK * ncp * b, d)


def _from_token_major(y, b, seq, ncp):
    d = y.shape[1]
    nc = seq // CHUNK
    y4 = y.reshape(CHUNK, ncp, b, d)[:, 1:1 + nc]
    return y4.transpose(2, 1, 0, 3).reshape(b, seq, d)


def kernel(x, meta_tokens, norm_mix, norm_ffn, norm_final, s5_w_in, s5_a_re, s5_a_im, s5_log_dt,
           s5_b_re, s5_b_im, s5_c_re, s5_c_im, s5_d, s5_w_glu, pool_w_in, pool_w_grp, pool_scale,
           moe_w_coarse, moe_b_coarse, moe_w_fine, moe_b_fine, moe_w_gate, moe_w_up, moe_w_down):
    b, seq, d = x.shape
    depth = norm_mix.shape[0]
    assert meta_tokens.shape[0] == N_META == CHUNK and seq % CHUNK == 0
    assert LANE % b == 0 and b % 16 == 0
    per_lane = LANE // b
    ncp = -(-(seq // CHUNK + 1) // per_lane) * per_lane
    slp = ncp * b

    h = _to_token_major(x, meta_tokens, ncp)
    hn = _rmsnorm(h, norm_mix[0], BF16)
    for i in range(depth):
        j = i // 2
        if i % 2 == 0:
            g = s5_a_re.shape[1]
            ops = _s5_prep(s5_a_re[j], s5_a_im[j], s5_log_dt[j], s5_b_re[j], s5_b_im[j],
                           s5_c_re[j], s5_c_im[j])
            d_b = jnp.broadcast_to(jnp.tile(s5_d[j].reshape(g, 1, S5_GROUP), (1, CHUNK, 1))
                                   .reshape(g, CHUNK_W, 1), (g, CHUNK_W, LANE))
            ut = _s5_in(hn, s5_w_in[j].T.astype(BF16), slp)
            zt = _s5_core(ut, ops, d_b, b, ncp)
            h = _s5_out(zt, s5_w_glu[j].astype(BF16), h, slp)
        else:
            u = _matmul(hn, pool_w_in[j].astype(BF16), F32)
            h = _pool_core(u, pool_w_grp[j].astype(BF16), pool_scale[j], h, b, slp)
        last = i == depth - 1
        g_next = norm_final if last else norm_mix[i + 1]
        h, hn = _hier_moe(h, norm_ffn[i], g_next, F32 if last else BF16,
                          moe_w_coarse[i], moe_b_coarse[i], moe_w_fine[i], moe_b_fine[i],
                          moe_w_gate[i], moe_w_up[i], moe_w_down[i])
    return _from_token_major(hn, b, seq, ncp)
```

```python
import functools

import jax
import jax.numpy as jnp
from jax import lax
from jax.experimental import pallas as pl
from jax.experimental.pallas import tpu as pltpu

F32 = jnp.float32
BF16 = jnp.bfloat16
I32 = jnp.int32

N_META = 16
CHUNK = 16
RMS_EPS = 1e-6
S5_GROUP = 16
S5_STATE = 64
CHUNK_W = CHUNK * S5_GROUP
POOL_WINDOWS = (2, 4, 8, 16)
N_EXPERT_GROUPS = 4
EXPERTS_PER_GROUP = 8
N_EXPERTS = N_EXPERT_GROUPS * EXPERTS_PER_GROUP
ROUTER_ROWS = 128
FINE_ROW0 = 8

LANE = 128
TOK_TILE = 256
EXP_TILE = 256
VMEM_LIMIT = 56 * 1024 * 1024

_NT = (((1,), (1,)), ((), ()))
_TN = (((0,), (0,)), ((), ()))


def _cparams(*sem):
    return pltpu.CompilerParams(dimension_semantics=sem, vmem_limit_bytes=VMEM_LIMIT)


def _rms(x, g):
    return x * lax.rsqrt(jnp.mean(x * x, axis=-1, keepdims=True) + RMS_EPS) * g


def _pack_pairs(lo, hi):
    return pltpu.pack_elementwise([lo, hi], packed_dtype=BF16)


def _unpack_pairs(w):
    lo = pltpu.unpack_elementwise(w, index=0, packed_dtype=BF16, unpacked_dtype=F32)
    hi = pltpu.unpack_elementwise(w, index=1, packed_dtype=BF16, unpacked_dtype=F32)
    return lo, hi


def _rmsnorm_kernel(h_ref, g_ref, o_ref):
    o_ref[...] = _rms(h_ref[...], g_ref[...]).astype(o_ref.dtype)


def _rmsnorm(h, g, out_dtype):
    n, d = h.shape
    return pl.pallas_call(
        _rmsnorm_kernel,
        grid=(n // TOK_TILE,),
        in_specs=[pl.BlockSpec((TOK_TILE, d), lambda i: (i, 0)),
                  pl.BlockSpec((1, d), lambda i: (0, 0))],
        out_specs=pl.BlockSpec((TOK_TILE, d), lambda i: (i, 0)),
        out_shape=jax.ShapeDtypeStruct((n, d), out_dtype),
        compiler_params=_cparams("parallel"),
        name="rmsnorm",
    )(h, g.reshape(1, d))


def _mm_kernel(x_ref, w_ref, o_ref):
    o_ref[...] = jnp.dot(x_ref[...], w_ref[...], preferred_element_type=F32).astype(o_ref.dtype)


def _matmul(x, w, out_dtype, tm=512):
    n, k = x.shape
    f = w.shape[1]
    return pl.pallas_call(
        _mm_kernel,
        grid=(n // tm,),
        in_specs=[pl.BlockSpec((tm, k), lambda i: (i, 0)),
                  pl.BlockSpec((k, f), lambda i: (0, 0))],
        out_specs=pl.BlockSpec((tm, f), lambda i: (i, 0)),
        out_shape=jax.ShapeDtypeStruct((n, f), out_dtype),
        compiler_params=_cparams("parallel"),
        name="pool_in",
    )(x, w)


def _mm_nt_kernel(w_ref, x_ref, o_ref):
    o_ref[...] = lax.dot_general(w_ref[...], x_ref[...], _NT,
                                 preferred_element_type=F32).astype(o_ref.dtype)


def _s5_in(hn, wt, slp):
    n, k = hn.shape
    f = wt.shape[0]
    tf = min(512, f)
    return pl.pallas_call(
        _mm_nt_kernel,
        grid=(CHUNK, f // tf),
        in_specs=[pl.BlockSpec((tf, k), lambda t, j: (j, 0)),
                  pl.BlockSpec((None, slp, k), lambda t, j: (t, 0, 0))],
        out_specs=pl.BlockSpec((None, tf, slp), lambda t, j: (t, j, 0)),
        out_shape=jax.ShapeDtypeStruct((CHUNK, f, slp), BF16),
        compiler_params=_cparams("parallel", "arbitrary"),
        name="s5_in",
    )(wt, hn.reshape(CHUNK, slp, k))


def _s5_prep_kernel(lrc_ref, lic_ref, ldt_ref, brt_ref, bit_ref, cre_ref, cim_ref,
                    lrr_ref, lir_ref, crt_ref, cit_ref,
                    m_ref, wz_ref, wyre_ref, wyim_ref, a16re_ref, a16im_ref):
    g = pl.program_id(0)
    p = S5_STATE
    dt = jnp.exp(ldt_ref[0:1, :])
    lr = lrc_ref[...]
    li = lic_ref[...]
    lam_r = lr * dt
    lam_i = li * dt
    mag = jnp.exp(lam_r)
    abr = mag * jnp.cos(lam_i)
    abi = mag * jnp.sin(lam_i)
    den = lr * lr + li * li
    nr = abr - 1.0
    cfr = (nr * lr + abi * li) / den
    cfi = (abi * lr - nr * li) / den

    def two(x):
        return jnp.concatenate([x, x], axis=1)

    cfr2, cfi2, lamr2, lami2 = two(cfr), two(cfi), two(lam_r), two(lam_i)
    brt = brt_ref[...]
    bit = bit_ref[...]
    bbr = cfr2 * brt - cfi2 * bit
    bbi = cfr2 * bit + cfi2 * brt
    lane = lax.broadcasted_iota(I32, (1, CHUNK_W), 1)
    k = (CHUNK - 1 - (lane >> 4)).astype(F32)
    er = jnp.exp(k * lamr2)
    pr = er * jnp.cos(k * lami2)
    pi = er * jnp.sin(k * lami2)
    wzr = pr * bbr - pi * bbi
    wzi = pr * bbi + pi * bbr
    wz_ref[0:p, :] = wzr.astype(BF16)
    wz_ref[p:2 * p, :] = wzi.astype(BF16)

    krev = (jnp.dot(cre_ref[...], wzr, precision=lax.Precision.HIGHEST, preferred_element_type=F32)
            - jnp.dot(cim_ref[...], wzi, precision=lax.Precision.HIGHEST, preferred_element_type=F32))
    for t in range(CHUNK):
        s = (CHUNK - 1 - t) * S5_GROUP
        blk = krev if s == 0 else pltpu.roll(krev, CHUNK_W - s, 1)
        blk = jnp.where(lane < CHUNK_W - s, blk, 0.0)
        m_ref[t * S5_GROUP:(t + 1) * S5_GROUP, :] = blk.astype(BF16)

    row = lax.broadcasted_iota(I32, (CHUNK_W, 1), 0)
    kk = ((row >> 4) + 1).astype(F32)
    lamr_row = lrr_ref[0:1, :] * dt
    lami_row = lir_ref[0:1, :] * dt
    erow = jnp.exp(kk * lamr_row)
    pwr = erow * jnp.cos(kk * lami_row)
    pwi = erow * jnp.sin(kk * lami_row)
    cr2 = crt_ref[...]
    ci2 = cit_ref[...]
    lane128 = lax.broadcasted_iota(I32, (1, LANE), 1)
    sel = (lane128 >> 6) == (g % 2)
    wyre_ref[...] = jnp.where(sel, cr2 * pwr - ci2 * pwi, 0.0).astype(BF16)
    wyim_ref[...] = jnp.where(sel, -(cr2 * pwi) - ci2 * pwr, 0.0).astype(BF16)

    e16 = jnp.exp(float(CHUNK) * lamr_row)
    a16re_ref[...] = jnp.broadcast_to(e16 * jnp.cos(float(CHUNK) * lami_row), (8, LANE))
    a16im_ref[...] = jnp.broadcast_to(e16 * jnp.sin(float(CHUNK) * lami_row), (8, LANE))


def _s5_prep(a_re, a_im, log_dt, b_re, b_im, c_re, c_im):
    g, p = a_re.shape
    lrc = jnp.broadcast_to(a_re[:, :, None], (g, p, LANE))
    lic = jnp.broadcast_to(a_im[:, :, None], (g, p, LANE))
    ldt = jnp.broadcast_to(log_dt[:, None, None], (g, 8, LANE))
    brt = jnp.tile(b_re, (1, 1, CHUNK))
    bit = jnp.tile(b_im, (1, 1, CHUNK))
    lrr = jnp.broadcast_to(jnp.concatenate([a_re, a_re], axis=1)[:, None, :], (g, 8, LANE))
    lir = jnp.broadcast_to(jnp.concatenate([a_im, a_im], axis=1)[:, None, :], (g, 8, LANE))
    crt = jnp.tile(jnp.concatenate([c_re, c_re], axis=2), (1, CHUNK, 1))
    cit = jnp.tile(jnp.concatenate([c_im, c_im], axis=2), (1, CHUNK, 1))

    def spec(r, c):
        return pl.BlockSpec((None, r, c), lambda i: (i, 0, 0))

    return pl.pallas_call(
        _s5_prep_kernel,
        grid=(g,),
        in_specs=[spec(p, LANE), spec(p, LANE), spec(8, LANE), spec(p, CHUNK_W), spec(p, CHUNK_W),
                  spec(S5_GROUP, p), spec(S5_GROUP, p), spec(8, LANE), spec(8, LANE),
                  spec(CHUNK_W, LANE), spec(CHUNK_W, LANE)],
        out_specs=[spec(CHUNK_W, CHUNK_W), spec(2 * p, CHUNK_W), spec(CHUNK_W, LANE),
                   spec(CHUNK_W, LANE), spec(8, LANE), spec(8, LANE)],
        out_shape=[jax.ShapeDtypeStruct((g, CHUNK_W, CHUNK_W), BF16),
                   jax.ShapeDtypeStruct((g, 2 * p, CHUNK_W), BF16),
                   jax.ShapeDtypeStruct((g, CHUNK_W, LANE), BF16),
                   jax.ShapeDtypeStruct((g, CHUNK_W, LANE), BF16),
                   jax.ShapeDtypeStruct((g, 8, LANE), F32),
                   jax.ShapeDtypeStruct((g, 8, LANE), F32)],
        compiler_params=_cparams("parallel"),
        name="s5_prep",
    )(lrc, lic, ldt, brt, bit, c_re, c_im, lrr, lir, crt, cit)


def _gelu_tanh(x):
    return x * (0.5 * (1.0 + jnp.tanh(0.7978845608028654 * (x + 0.044715 * (x * x * x)))))


def _s5_core_kernel(ut_ref, m_ref, wz_ref, wyre_ref, wyim_ref, are_ref, aim_ref, d_ref,
                    o_ref, zre_s, zim_s, spre_s, spim_s, *, nb, ncp):
    p = S5_STATE
    slp = nb * ncp
    vs, zs = [], []
    for q in range(2):
        v = ut_ref[:, q * S5_GROUP:(q + 1) * S5_GROUP, :].reshape(CHUNK_W, slp)
        vs.append(v)
        zs.append(jnp.dot(wz_ref[q], v, preferred_element_type=F32))
    zre_s[...] = jnp.concatenate([zs[0][0:p], zs[1][0:p]], axis=0).T
    zim_s[...] = jnp.concatenate([zs[0][p:2 * p], zs[1][p:2 * p]], axis=0).T

    lane = lax.broadcasted_iota(I32, (1, LANE), 1)
    are = jnp.where(lane < p, are_ref[0, 0:1, :], are_ref[1, 0:1, :])
    aim = jnp.where(lane < p, aim_ref[0, 0:1, :], aim_ref[1, 0:1, :])

    def step(c, carry):
        sr, si = carry
        r0 = pl.multiple_of(c * nb, nb)
        spre_s[pl.ds(r0, nb), :] = sr.astype(BF16)
        spim_s[pl.ds(r0, nb), :] = si.astype(BF16)
        zr = zre_s[pl.ds(r0, nb), :]
        zi = zim_s[pl.ds(r0, nb), :]
        return are * sr - aim * si + zr, are * si + aim * sr + zi

    zero = jnp.zeros((nb, LANE), F32)
    lax.fori_loop(0, ncp, step, (zero, zero))

    spre = spre_s[...]
    spim = spim_s[...]
    for q in range(2):
        y_in = jnp.dot(m_ref[q], vs[q], preferred_element_type=F32)
        y_st = (lax.dot_general(wyre_ref[q], spre, _NT, preferred_element_type=F32)
                + lax.dot_general(wyim_ref[q], spim, _NT, preferred_element_type=F32))
        d = d_ref[q]
        for jb in range(slp // LANE):
            sl = slice(jb * LANE, (jb + 1) * LANE)
            y = y_in[:, sl] + y_st[:, sl] + d * vs[q][:, sl].astype(F32)
            z = _gelu_tanh(y).astype(BF16)
            o_ref[:, q * S5_GROUP:(q + 1) * S5_GROUP, sl] = z.reshape(CHUNK, S5_GROUP, LANE)


def _s5_core(ut, ops, d_b, nb, ncp):
    m, wz, wyre, wyim, a16re, a16im = ops
    _, f, slp = ut.shape
    g = f // S5_GROUP

    def pair(r, c):
        return pl.BlockSpec((2, r, c), lambda i: (i, 0, 0))

    io_spec = pl.BlockSpec((CHUNK, 2 * S5_GROUP, slp), lambda i: (0, i, 0))
    return pl.pallas_call(
        functools.partial(_s5_core_kernel, nb=nb, ncp=ncp),
        grid=(g // 2,),
        in_specs=[io_spec, pair(CHUNK_W, CHUNK_W), pair(2 * S5_STATE, CHUNK_W),
                  pair(CHUNK_W, LANE), pair(CHUNK_W, LANE), pair(8, LANE), pair(8, LANE),
                  pair(CHUNK_W, LANE)],
        out_specs=io_spec,
        out_shape=jax.ShapeDtypeStruct(ut.shape, BF16),
        scratch_shapes=[pltpu.VMEM((slp, LANE), F32), pltpu.VMEM((slp, LANE), F32),
                        pltpu.VMEM((slp, LANE), BF16), pltpu.VMEM((slp, LANE), BF16)],
        compiler_params=_cparams("parallel"),
        name="s5_core",
    )(ut, m, wz, wyre, wyim, a16re, a16im, d_b)


def _s5_out_kernel(z_ref, wv_ref, wg_ref, h_ref, o_ref, zt_s):
    @pl.when(pl.program_id(1) == 0)
    def _():
        zt_s[...] = z_ref[...].T

    z = zt_s[...]
    v = jnp.dot(z, wv_ref[...], preferred_element_type=F32)
    gate = jnp.dot(z, wg_ref[...], preferred_element_type=F32)
    o_ref[...] = h_ref[...] + v * jax.nn.sigmoid(gate)


def _s5_out(zt, w_glu, h, slp):
    _, f, _ = zt.shape
    n, d = h.shape
    tn = min(256, d)
    nj = d // tn
    h3 = h.reshape(CHUNK, slp, d)
    out = pl.pallas_call(
        _s5_out_kernel,
        grid=(CHUNK, nj),
        in_specs=[pl.BlockSpec((None, f, slp), lambda t, j: (t, 0, 0)),
                  pl.BlockSpec((f, tn), lambda t, j: (0, j)),
                  pl.BlockSpec((f, tn), lambda t, j: (0, nj + j)),
                  pl.BlockSpec((None, slp, tn), lambda t, j: (t, 0, j))],
        out_specs=pl.BlockSpec((None, slp, tn), lambda t, j: (t, 0, j)),
        out_shape=jax.ShapeDtypeStruct(h3.shape, F32),
        scratch_shapes=[pltpu.VMEM((slp, f), BF16)],
        compiler_params=_cparams("parallel", "arbitrary"),
        name="s5_out",
    )(zt, w_glu, w_glu, h3)
    return out.reshape(n, d)


def _pool_body(win, first, u_ref, halo_ref, w_ref, sc_ref, h_ref, o_ref, nb):
    rows = u_ref.shape[1]
    row = lax.broadcasted_iota(I32, (rows, 1), 0)

    def cur(t):
        return u_ref[t]

    def prev(t):
        halo = jnp.where(first, 0.0, halo_ref[t])
        return jnp.concatenate([halo, u_ref[t, 0:rows - nb, :]], axis=0)

    def at(tau):
        return cur(tau) if tau >= 0 else prev(tau + CHUNK)

    run = at(0)
    for j in range(1, win):
        run = run + at(-j)
    wmat = w_ref[...]
    scale = sc_ref[...]
    for t in range(CHUNK):
        if t > 0:
            run = run + at(t) - at(t - win)
        if t + 1 >= win:
            inv = 1.0 / win
        else:
            inv = jnp.where(jnp.logical_and(first, row < nb), 1.0 / (t + 1), 1.0 / win)
        mixed = run * inv - cur(t)
        y = jnp.dot(mixed.astype(BF16), wmat, preferred_element_type=F32)
        o_ref[t] = h_ref[t] + y * scale


def _pool_kernel(u_ref, halo_ref, w_ref, sc_ref, h_ref, o_ref, *, nb):
    g = pl.program_id(0)
    first = pl.program_id(1) == 0
    for gi, win in enumerate(POOL_WINDOWS):
        @pl.when(g == gi)
        def _(win=win):
            _pool_body(win, first, u_ref, halo_ref, w_ref, sc_ref, h_ref, o_ref, nb)


def _pool_core(u, w_grp, scale, h, nb, slp):
    n, d = h.shape
    ng = len(POOL_WINDOWS)
    cw = d // ng
    rt = LANE
    per = rt // nb
    u3 = u.reshape(CHUNK, slp, d)
    h3 = h.reshape(CHUNK, slp, d)
    blk = pl.BlockSpec((CHUNK, rt, cw), lambda g, i: (0, i, g))
    out = pl.pallas_call(
        functools.partial(_pool_kernel, nb=nb),
        grid=(ng, slp // rt),
        in_specs=[blk,
                  pl.BlockSpec((CHUNK, nb, cw), lambda g, i: (0, jnp.maximum(i * per - 1, 0), g)),
                  pl.BlockSpec((None, cw, cw), lambda g, i: (g, 0, 0)),
                  pl.BlockSpec((1, cw), lambda g, i: (0, g)),
                  blk],
        out_specs=blk,
        out_shape=jax.ShapeDtypeStruct(h3.shape, F32),
        compiler_params=_cparams("parallel", "parallel"),
        name="pool_core",
    )(u3, u3, w_grp, scale.reshape(1, d), h3)
    return out.reshape(n, d)


def _router_kernel(h_ref, g_ref, whi_ref, wlo_ref, b_ref, eid_ref, wts_ref, rank_ref, cnt_ref, carry_s):
    i = pl.program_id(0)
    tm = h_ref.shape[0]

    @pl.when(i == 0)
    def _():
        carry_s[...] = jnp.zeros_like(carry_s)

    xn = _rms(h_ref[...], g_ref[...])
    x_hi = xn.astype(BF16)
    x_lo = (xn - x_hi.astype(F32)).astype(BF16)
    w_hi = whi_ref[...]
    lt = (jnp.dot(x_hi, w_hi, preferred_element_type=F32)
          + jnp.dot(x_lo, w_hi, preferred_element_type=F32)
          + jnp.dot(x_hi, wlo_ref[...], preferred_element_type=F32))
    logits = lt.T + b_ref[:, 0:1]

    c = [logits[r:r + 1] for r in range(N_EXPERT_GROUPS)]
    cmax = jnp.maximum(jnp.maximum(c[0], c[1]), jnp.maximum(c[2], c[3]))
    grp = jnp.where(c[0] == cmax, 0, jnp.where(c[1] == cmax, 1, jnp.where(c[2] == cmax, 2, 3)))
    csum = (jnp.exp(c[0] - cmax) + jnp.exp(c[1] - cmax)) + (jnp.exp(c[2] - cmax) + jnp.exp(c[3] - cmax))
    p_grp = 1.0 / csum

    epg = EXPERTS_PER_GROUP
    sel = jnp.zeros((epg, tm), F32)
    for gi in range(N_EXPERT_GROUPS):
        blk = logits[FINE_ROW0 + gi * epg:FINE_ROW0 + (gi + 1) * epg]
        sel = jnp.where(grp == gi, blk, sel)
    ridx = lax.broadcasted_iota(I32, (epg, tm), 0)
    m1 = jnp.max(sel, axis=0, keepdims=True)
    i1 = jnp.min(jnp.where(sel == m1, ridx, epg), axis=0, keepdims=True)
    sel2 = jnp.where(ridx == i1, -jnp.inf, sel)
    m2 = jnp.max(sel2, axis=0, keepdims=True)
    i2 = jnp.min(jnp.where(sel2 == m2, ridx, epg), axis=0, keepdims=True)
    ssum = jnp.sum(jnp.exp(sel - m1), axis=0, keepdims=True)
    p1 = 1.0 / ssum
    p2 = jnp.exp(m2 - m1) / ssum
    den = p1 + p2
    e1 = grp * epg + i1
    e2 = grp * epg + i2
    eid_ref[0:1, :] = e1
    eid_ref[1:2, :] = e2
    wts_ref[0:1, :] = p1 / den * p_grp
    wts_ref[1:2, :] = p2 / den * p_grp

    eidx = lax.broadcasted_iota(I32, (N_EXPERTS, tm), 0)
    hit1 = eidx == e1
    hit2 = eidx == e2
    onehot = jnp.where(hit1, 1.0, jnp.where(hit2, 1.0, 0.0))
    tri = jnp.where(lax.broadcasted_iota(I32, (tm, tm), 0) <= lax.broadcasted_iota(I32, (tm, tm), 1),
                    1.0, 0.0).astype(BF16)
    pref = jnp.dot(onehot.astype(BF16), tri, preferred_element_type=F32)
    tot = pref + carry_s[:, 0:1]
    rank_ref[0:1, :] = (jnp.sum(jnp.where(hit1, tot, 0.0), axis=0, keepdims=True) - 1.0).astype(I32)
    rank_ref[1:2, :] = (jnp.sum(jnp.where(hit2, tot, 0.0), axis=0, keepdims=True) - 1.0).astype(I32)
    carry_s[...] = carry_s[...] + pref[:, tm - 1:tm]
    cnt_ref[...] = carry_s[...]


def _router(h, g, w_hi, w_lo, bias):
    n, d = h.shape
    tm = TOK_TILE
    two = pl.BlockSpec((2, tm), lambda i: (0, i))
    return pl.pallas_call(
        _router_kernel,
        grid=(n // tm,),
        in_specs=[pl.BlockSpec((tm, d), lambda i: (i, 0)),
                  pl.BlockSpec((1, d), lambda i: (0, 0)),
                  pl.BlockSpec((d, ROUTER_ROWS), lambda i: (0, 0)),
                  pl.BlockSpec((d, ROUTER_ROWS), lambda i: (0, 0)),
                  pl.BlockSpec((ROUTER_ROWS, LANE), lambda i: (0, 0))],
        out_specs=[two, two, two, pl.BlockSpec((N_EXPERTS, LANE), lambda i: (0, 0))],
        out_shape=[jax.ShapeDtypeStruct((2, n), I32), jax.ShapeDtypeStruct((2, n), F32),
                   jax.ShapeDtypeStruct((2, n), I32), jax.ShapeDtypeStruct((N_EXPERTS, LANE), F32)],
        scratch_shapes=[pltpu.VMEM((N_EXPERTS, LANE), F32)],
        compiler_params=_cparams("arbitrary"),
        name="moe_router",
    )(h, g.reshape(1, d), w_hi, w_lo, bias)


def _router_params(w_coarse, b_coarse, w_fine, b_fine):
    d = w_coarse.shape[0]
    pad0 = jnp.zeros((d, FINE_ROW0 - N_EXPERT_GROUPS), F32)
    pad1 = jnp.zeros((d, ROUTER_ROWS - FINE_ROW0 - N_EXPERTS), F32)
    wr = jnp.concatenate([w_coarse.astype(F32), pad0, w_fine.astype(F32), pad1], axis=1)
    w_hi = wr.astype(BF16)
    w_lo = (wr - w_hi.astype(F32)).astype(BF16)
    bias = jnp.concatenate([b_coarse.astype(F32), pad0[0], b_fine.astype(F32), pad1[0]])
    return w_hi, w_lo, jnp.broadcast_to(bias[:, None], (ROUTER_ROWS, LANE))


def _route_tables(counts, n_tiles):
    cnt = counts[:, 0].astype(I32)
    tiles_e = (cnt + EXP_TILE - 1) // EXP_TILE
    tile_end = jnp.cumsum(tiles_e)
    tile_start = tile_end - tiles_e
    n_used = tile_end[-1]
    tq = jnp.minimum(jnp.arange(n_tiles, dtype=I32), n_used - 1)
    te = jnp.sum((tile_end[None, :] <= tq[:, None]).astype(I32), axis=1)
    onehot = (te[:, None] == jnp.arange(N_EXPERTS, dtype=I32)[None, :]).astype(I32)
    cnt_t = jnp.sum(onehot * cnt[None, :], axis=1)
    start_t = jnp.sum(onehot * tile_start[None, :], axis=1)
    tid = jnp.arange(n_tiles, dtype=I32)
    valid = jnp.where(tid < n_used, jnp.clip(cnt_t - (tid - start_t) * EXP_TILE, 0, EXP_TILE), 0)
    seg_start = jnp.broadcast_to((tile_start * EXP_TILE)[:, None], (N_EXPERTS, LANE))
    return seg_start.astype(I32), te.astype(I32), valid.astype(I32), jnp.reshape(n_used, (1,)).astype(I32)


def _pos_kernel(eid_ref, rank_ref, st_ref, pos_ref):
    tm = eid_ref.shape[1]
    eidx = lax.broadcasted_iota(I32, (N_EXPERTS, tm), 0)
    st = st_ref[:, 0:1]
    for k in range(2):
        base = jnp.sum(jnp.where(eidx == eid_ref[k:k + 1, :], st, 0), axis=0, keepdims=True)
        pos_ref[k:k + 1, :] = base + rank_ref[k:k + 1, :]


def _positions(eid, rank, seg_start):
    n = eid.shape[1]
    tm = TOK_TILE
    two = pl.BlockSpec((2, tm), lambda i: (0, i))
    return pl.pallas_call(
        _pos_kernel,
        grid=(n // tm,),
        in_specs=[two, two, pl.BlockSpec((N_EXPERTS, LANE), lambda i: (0, 0))],
        out_specs=pl.BlockSpec((None, 2, tm), lambda i: (i, 0, 0)),
        out_shape=jax.ShapeDtypeStruct((n // tm, 2, tm), I32),
        compiler_params=_cparams("parallel"),
        name="moe_positions",
    )(eid, rank, seg_start)


def _store_row_tiles(dst, words):
    for j in range(dst.shape[1]):
        dst[:, j, :] = words[:, j * LANE:(j + 1) * LANE]


def _load_row_tiles(src):
    return jnp.concatenate([src[:, j, :] for j in range(src.shape[1])], axis=1)


def _dispatch_kernel(pos_ref, h_ref, g_ref, xs_in, xs_ref, buf, sem):
    del xs_in
    i = pl.program_id(0)
    n = pl.num_programs(0)
    tm = h_ref.shape[0]
    half = h_ref.shape[1] // 2
    slot = i % 2

    def wait_slot(s):
        for _ in range(2):
            pltpu.make_async_copy(buf.at[s], xs_ref.at[pl.ds(0, tm)], sem.at[s]).wait()

    @pl.when(i >= 2)
    def _():
        wait_slot(slot)

    xn = _rms(h_ref[...], g_ref[...])
    _store_row_tiles(buf.at[slot], _pack_pairs(xn[:, :half], xn[:, half:]))

    def issue(r, carry):
        for k in range(2):
            pltpu.make_async_copy(buf.at[slot, r], xs_ref.at[pos_ref[k, r]], sem.at[slot]).start(priority=k)
        return carry

    lax.fori_loop(0, tm, issue, 0, unroll=8)

    @pl.when(i == n - 1)
    def _():
        wait_slot(slot)

        @pl.when(n >= 2)
        def _():
            wait_slot(1 - slot)


def _dispatch(h, g, pos3, n_rows, word_dtype):
    n, d = h.shape
    tm = TOK_TILE
    nl = d // 2 // LANE
    xs0 = jnp.zeros((n_rows, nl, LANE), word_dtype)
    return pl.pallas_call(
        _dispatch_kernel,
        grid=(n // tm,),
        in_specs=[pl.BlockSpec((None, 2, tm), lambda i: (i, 0, 0), memory_space=pltpu.SMEM),
                  pl.BlockSpec((tm, d), lambda i: (i, 0)),
                  pl.BlockSpec((1, d), lambda i: (0, 0)),
                  pl.BlockSpec(memory_space=pl.ANY)],
        out_specs=pl.BlockSpec(memory_space=pl.ANY),
        out_shape=jax.ShapeDtypeStruct((n_rows, nl, LANE), word_dtype),
        scratch_shapes=[pltpu.VMEM((2, tm, nl, LANE), word_dtype), pltpu.SemaphoreType.DMA((2,))],
        input_output_aliases={3: 0},
        compiler_params=_cparams("arbitrary"),
        name="moe_dispatch",
    )(pos3, h, g.reshape(1, d), xs0)


def _expert_kernel(te_ref, va_ref, nu_ref, xs_ref, wg_ref, wu_ref, wd_ref, ys_ref, wgb, wub, wdb):
    del nu_ref
    i = pl.program_id(0)
    changed = jnp.logical_or(i == 0, te_ref[i] != te_ref[jnp.maximum(i - 1, 0)])

    @pl.when(changed)
    def _():
        wgb[...] = wg_ref[...].astype(BF16)
        wub[...] = wu_ref[...].astype(BF16)
        wdb[...] = wd_ref[...].astype(BF16)

    valid = va_ref[i]

    @pl.when(valid > 0)
    def _():
        lo, hi = _unpack_pairs(_load_row_tiles(xs_ref))
        lo = lo.astype(BF16)
        hi = hi.astype(BF16)
        half = lo.shape[1]
        gate = (jnp.dot(lo, wgb[0:half, :], preferred_element_type=F32)
                + jnp.dot(hi, wgb[half:2 * half, :], preferred_element_type=F32))
        up = (jnp.dot(lo, wub[0:half, :], preferred_element_type=F32)
              + jnp.dot(hi, wub[half:2 * half, :], preferred_element_type=F32))
        hid = (gate * jax.nn.sigmoid(gate)) * up
        y = jnp.dot(hid.astype(BF16), wdb[...], preferred_element_type=F32)
        _store_row_tiles(ys_ref, _pack_pairs(y[:, :half], y[:, half:]))

    @pl.when(valid <= 0)
    def _():
        ys_ref[...] = jnp.zeros_like(ys_ref)


def _experts(xs, te, valid, n_used, w_gate, w_up, w_down):
    n_rows, nl, _ = xs.shape
    _, d, de = w_gate.shape
    te_tile = EXP_TILE
    grid_spec = pltpu.PrefetchScalarGridSpec(
        num_scalar_prefetch=3,
        grid=(n_rows // te_tile,),
        in_specs=[pl.BlockSpec((te_tile, nl, LANE), lambda i, te, va, nu: (jnp.minimum(i, nu[0] - 1), 0, 0)),
                  pl.BlockSpec((None, d, de), lambda i, te, va, nu: (te[i], 0, 0)),
                  pl.BlockSpec((None, d, de), lambda i, te, va, nu: (te[i], 0, 0)),
                  pl.BlockSpec((None, de, d), lambda i, te, va, nu: (te[i], 0, 0))],
        out_specs=pl.BlockSpec((te_tile, nl, LANE), lambda i, te, va, nu: (i, 0, 0)),
        scratch_shapes=[pltpu.VMEM((d, de), BF16), pltpu.VMEM((d, de), BF16), pltpu.VMEM((de, d), BF16)],
    )
    return pl.pallas_call(
        _expert_kernel,
        grid_spec=grid_spec,
        out_shape=jax.ShapeDtypeStruct(xs.shape, xs.dtype),
        compiler_params=_cparams("arbitrary"),
        name="moe_experts",
    )(te, valid, n_used, xs, w_gate, w_up, w_down)


def _combine_kernel(pos_ref, posn_ref, w_ref, h_ref, g_ref, ys_ref, o_ref, hn_ref, buf, sem):
    i = pl.program_id(0)
    n = pl.num_programs(0)
    tm = h_ref.shape[0]
    half = h_ref.shape[1] // 2
    slot = i % 2

    def issue(p_ref, s):
        def body(r, carry):
            for k in range(2):
                pltpu.make_async_copy(ys_ref.at[p_ref[k, r]], buf.at[s, k, r], sem.at[s]).start(priority=k)
            return carry
        lax.fori_loop(0, tm, body, 0, unroll=8)

    @pl.when(i == 0)
    def _():
        issue(pos_ref, 0)

    @pl.when(i + 1 < n)
    def _():
        issue(posn_ref, 1 - slot)

    for k in range(2):
        pltpu.make_async_copy(ys_ref.at[pl.ds(0, tm)], buf.at[slot, k], sem.at[slot]).wait()

    lo0, hi0 = _unpack_pairs(_load_row_tiles(buf.at[slot, 0]))
    lo1, hi1 = _unpack_pairs(_load_row_tiles(buf.at[slot, 1]))
    w0 = jnp.broadcast_to(w_ref[0:1, :], (LANE, tm)).T[:, 0:1]
    w1 = jnp.broadcast_to(w_ref[1:2, :], (LANE, tm)).T[:, 0:1]
    h = h_ref[...]
    hnew = jnp.concatenate([h[:, :half] + (w0 * lo0 + w1 * lo1),
                            h[:, half:] + (w0 * hi0 + w1 * hi1)], axis=1)
    o_ref[...] = hnew
    hn_ref[...] = _rms(hnew, g_ref[...]).astype(hn_ref.dtype)


def _combine(h, ys, pos3, wts, g_next, hn_dtype):
    n, d = h.shape
    tm = TOK_TILE
    nt = n // tm
    nl = ys.shape[1]
    smem = functools.partial(pl.BlockSpec, (None, 2, tm), memory_space=pltpu.SMEM)
    row = pl.BlockSpec((tm, d), lambda i: (i, 0))
    return pl.pallas_call(
        _combine_kernel,
        grid=(nt,),
        in_specs=[smem(lambda i: (i, 0, 0)),
                  smem(lambda i: (jnp.minimum(i + 1, nt - 1), 0, 0)),
                  pl.BlockSpec((2, tm), lambda i: (0, i)),
                  row,
                  pl.BlockSpec((1, d), lambda i: (0, 0)),
                  pl.BlockSpec(memory_space=pl.ANY)],
        out_specs=[row, row],
        out_shape=[jax.ShapeDtypeStruct((n, d), F32), jax.ShapeDtypeStruct((n, d), hn_dtype)],
        scratch_shapes=[pltpu.VMEM((2, 2, tm, nl, LANE), ys.dtype), pltpu.SemaphoreType.DMA((2,))],
        compiler_params=_cparams("arbitrary"),
        name="moe_combine",
    )(pos3, pos3, wts, h, g_next.reshape(1, d), ys)


def _hier_moe(h, g_ffn, g_next, hn_dtype, w_coarse, b_coarse, w_fine, b_fine, w_gate, w_up, w_down):
    n = h.shape[0]
    n_tiles = (2 * n + N_EXPERTS * (EXP_TILE - 1) + EXP_TILE - 1) // EXP_TILE
    w_hi, w_lo, bias = _router_params(w_coarse, b_coarse, w_fine, b_fine)
    eid, wts, rank, counts = _router(h, g_ffn, w_hi, w_lo, bias)
    seg_start, te, valid, n_used = _route_tables(counts, n_tiles)
    pos3 = _positions(eid, rank, seg_start)
    word_dtype = jax.eval_shape(lambda a: _pack_pairs(a, a), jax.ShapeDtypeStruct((8, LANE), F32)).dtype
    xs = _dispatch(h, g_ffn, pos3, n_tiles * EXP_TILE, word_dtype)
    ys = _experts(xs, te, valid, n_used, w_gate, w_up, w_down)
    return _combine(h, ys, pos3, wts, g_next, hn_dtype)


def _to_token_major(x, meta, ncp):
    b, seq, d = x.shape
    nc = seq // CHUNK
    xr = x.reshape(b, nc, CHUNK, d).transpose(2, 1, 0, 3)
    m = jnp.broadcast_to(meta[:, None, None, :].astype(x.dtype), (CHUNK, 1, b, d))
    pad = jnp.zeros((CHUNK, ncp - nc - 1, b, d), x.dtype)
    return jnp.concatenate([m, xr, pad], axis=1).reshape(CHUNK * ncp * b, d)


def _from_token_major(y, b, seq, ncp):
    d = y.shape[1]
    nc = seq // CHUNK
    y4 = y.reshape(CHUNK, ncp, b, d)[:, 1:1 + nc]
    return y4.transpose(2, 1, 0, 3).reshape(b, seq, d)


def kernel(x, meta_tokens, norm_mix, norm_ffn, norm_final, s5_w_in, s5_a_re, s5_a_im, s5_log_dt,
           s5_b_re, s5_b_im, s5_c_re, s5_c_im, s5_d, s5_w_glu, pool_w_in, pool_w_grp, pool_scale,
           moe_w_coarse, moe_b_coarse, moe_w_fine, moe_b_fine, moe_w_gate, moe_w_up, moe_w_down):
    b, seq, d = x.shape
    depth = norm_mix.shape[0]
    assert meta_tokens.shape[0] == N_META == CHUNK and seq % CHUNK == 0
    assert LANE % b == 0 and b % 16 == 0
    per_lane = LANE // b
    ncp = -(-(seq // CHUNK + 1) // per_lane) * per_lane
    slp = ncp * b

    h = _to_token_major(x, meta_tokens, ncp)
    hn = _rmsnorm(h, norm_mix[0], BF16)
    for i in range(depth):
        j = i // 2
        if i % 2 == 0:
            g = s5_a_re.shape[1]
            ops = _s5_prep(s5_a_re[j], s5_a_im[j], s5_log_dt[j], s5_b_re[j], s5_b_im[j],
                           s5_c_re[j], s5_c_im[j])
            d_b = jnp.broadcast_to(jnp.tile(s5_d[j].reshape(g, 1, S5_GROUP), (1, CHUNK, 1))
                                   .reshape(g, CHUNK_W, 1), (g, CHUNK_W, LANE))
            ut = _s5_in(hn, s5_w_in[j].T.astype(BF16), slp)
            zt = _s5_core(ut, ops, d_b, b, ncp)
            h = _s5_out(zt, s5_w_glu[j].astype(BF16), h, slp)
        else:
            u = _matmul(hn, pool_w_in[j].astype(BF16), F32)
            h = _pool_core(u, pool_w_grp[j].astype(BF16), pool_scale[j], h, b, slp)
        last = i == depth - 1
        g_next = norm_final if last else norm_mix[i + 1]
        h, hn = _hier_moe(h, norm_ffn[i], g_next, F32 if last else BF16,
                          moe_w_coarse[i], moe_b_coarse[i], moe_w_fine[i], moe_b_fine[i],
                          moe_w_gate[i], moe_w_up[i], moe_w_down[i])
    return _from_token_major(hn, b, seq, ncp)
```

```python
import functools

import jax
import jax.numpy as jnp
from jax import lax
from jax.experimental import pallas as pl
from jax.experimental.pallas import tpu as pltpu

F32 = jnp.float32
BF16 = jnp.bfloat16
I32 = jnp.int32

N_META = 16
CHUNK = 16
RMS_EPS = 1e-6
S5_GROUP = 16
S5_STATE = 64
CHUNK_W = CHUNK * S5_GROUP
POOL_WINDOWS = (2, 4, 8, 16)
N_EXPERT_GROUPS = 4
EXPERTS_PER_GROUP = 8
N_EXPERTS = N_EXPERT_GROUPS * EXPERTS_PER_GROUP
ROUTER_ROWS = 128
FINE_ROW0 = 8

LANE = 128
TOK_TILE = 256
EXP_TILE = 256
VMEM_LIMIT = 56 * 1024 * 1024

_NT = (((1,), (1,)), ((), ()))
_TN = (((0,), (0,)), ((), ()))


def _cparams(*sem):
    return pltpu.CompilerParams(dimension_semantics=sem, vmem_limit_bytes=VMEM_LIMIT)


def _rms(x, g):
    return x * lax.rsqrt(jnp.mean(x * x, axis=-1, keepdims=True) + RMS_EPS) * g


def _pack_pairs(lo, hi):
    return pltpu.pack_elementwise([lo, hi], packed_dtype=BF16)


def _unpack_pairs(w):
    lo = pltpu.unpack_elementwise(w, index=0, packed_dtype=BF16, unpacked_dtype=F32)
    hi = pltpu.unpack_elementwise(w, index=1, packed_dtype=BF16, unpacked_dtype=F32)
    return lo, hi


def _ingest_kernel(x_ref, meta_ref, g_ref, h_ref, hn_ref, *, nc):
    c = pl.program_id(0)
    g = g_ref[...]

    def emit(t, h):
        h_ref[t] = h
        hn_ref[t] = _rms(h, g).astype(hn_ref.dtype)

    @pl.when(c < nc)
    def _():
        for t in range(CHUNK):
            emit(t, x_ref[:, t, :])

    @pl.when(c == nc)
    def _():
        nb, d = h_ref.shape[1], h_ref.shape[2]
        for t in range(CHUNK):
            emit(t, jnp.broadcast_to(meta_ref[t:t + 1, :], (nb, d)))

    @pl.when(c > nc)
    def _():
        h_ref[...] = jnp.zeros_like(h_ref)
        hn_ref[...] = jnp.zeros_like(hn_ref)


def _ingest(x, meta, g, ncp):
    b, seq, d = x.shape
    nc = seq // CHUNK
    out = pl.BlockSpec((CHUNK, b, d), lambda c: (0, c, 0))
    h3, hn3 = pl.pallas_call(
        functools.partial(_ingest_kernel, nc=nc),
        grid=(ncp,),
        in_specs=[pl.BlockSpec((b, None, CHUNK, d), lambda c: (0, jnp.minimum(c, nc - 1), 0, 0)),
                  pl.BlockSpec((N_META, d), lambda c: (0, 0)),
                  pl.BlockSpec((1, d), lambda c: (0, 0))],
        out_specs=[out, out],
        out_shape=[jax.ShapeDtypeStruct((CHUNK, ncp * b, d), F32),
                   jax.ShapeDtypeStruct((CHUNK, ncp * b, d), BF16)],
        compiler_params=_cparams("parallel"),
        name="ingest",
    )(x.reshape(b, nc, CHUNK, d), meta.astype(F32), g.reshape(1, d))
    n = CHUNK * ncp * b
    return h3.reshape(n, d), hn3.reshape(n, d)


def _egress_kernel(h_ref, g_ref, o_ref):
    g = g_ref[...]
    for t in range(CHUNK):
        o_ref[:, t, :] = _rms(h_ref[t], g)


def _egress(h, g, b, seq, ncp):
    n, d = h.shape
    nc = seq // CHUNK
    out = pl.pallas_call(
        _egress_kernel,
        grid=(nc,),
        in_specs=[pl.BlockSpec((CHUNK, b, d), lambda c: (0, c, 0)),
                  pl.BlockSpec((1, d), lambda c: (0, 0))],
        out_specs=pl.BlockSpec((b, None, CHUNK, d), lambda c: (0, c, 0, 0)),
        out_shape=jax.ShapeDtypeStruct((b, nc, CHUNK, d), F32),
        compiler_params=_cparams("parallel"),
        name="egress",
    )(h.reshape(CHUNK, ncp * b, d), g.reshape(1, d))
    return out.reshape(b, seq, d)


def _mm_kernel(x_ref, w_ref, o_ref):
    o_ref[...] = jnp.dot(x_ref[...], w_ref[...], preferred_element_type=F32).astype(o_ref.dtype)


def _matmul(x, w, out_dtype, tm=512):
    n, k = x.shape
    f = w.shape[1]
    return pl.pallas_call(
        _mm_kernel,
        grid=(n // tm,),
        in_specs=[pl.BlockSpec((tm, k), lambda i: (i, 0)),
                  pl.BlockSpec((k, f), lambda i: (0, 0))],
        out_specs=pl.BlockSpec((tm, f), lambda i: (i, 0)),
        out_shape=jax.ShapeDtypeStruct((n, f), out_dtype),
        compiler_params=_cparams("parallel"),
        name="pool_in",
    )(x, w)


def _mm_t_kernel(x_ref, w_ref, o_ref):
    o_ref[...] = jnp.dot(x_ref[...], w_ref[...], preferred_element_type=F32).astype(o_ref.dtype).T


def _s5_in(hn, w, slp):
    n, k = hn.shape
    f = w.shape[1]
    tf = min(512, f)
    return pl.pallas_call(
        _mm_t_kernel,
        grid=(CHUNK, f // tf),
        in_specs=[pl.BlockSpec((None, slp, k), lambda t, j: (t, 0, 0)),
                  pl.BlockSpec((k, tf), lambda t, j: (0, j))],
        out_specs=pl.BlockSpec((None, tf, slp), lambda t, j: (t, j, 0)),
        out_shape=jax.ShapeDtypeStruct((CHUNK, f, slp), BF16),
        compiler_params=_cparams("parallel", "arbitrary"),
        name="s5_in",
    )(hn.reshape(CHUNK, slp, k), w)


def _s5_prep_kernel(lrc_ref, lic_ref, ldt_ref, brt_ref, bit_ref, cre_ref, cim_ref,
                    lrr_ref, lir_ref, crt_ref, cit_ref,
                    m_ref, wz_ref, wyre_ref, wyim_ref, a16re_ref, a16im_ref):
    g = pl.program_id(0)
    p = S5_STATE
    dt = jnp.exp(ldt_ref[0:1, :])
    lr = lrc_ref[...]
    li = lic_ref[...]
    lam_r = lr * dt
    lam_i = li * dt
    mag = jnp.exp(lam_r)
    abr = mag * jnp.cos(lam_i)
    abi = mag * jnp.sin(lam_i)
    den = lr * lr + li * li
    nr = abr - 1.0
    cfr = (nr * lr + abi * li) / den
    cfi = (abi * lr - nr * li) / den

    def two(x):
        return jnp.concatenate([x, x], axis=1)

    def powers(ar, ai, n):
        out = [(ar, ai)]
        for _ in range(n - 1):
            qr, qi = out[-1]
            out.append((qr * ar - qi * ai, qr * ai + qi * ar))
        return out

    cfr2, cfi2 = two(cfr), two(cfi)
    brt = brt_ref[...]
    bit = bit_ref[...]
    bbr = cfr2 * brt - cfi2 * bit
    bbi = cfr2 * bit + cfi2 * brt
    lane = lax.broadcasted_iota(I32, (1, CHUNK_W), 1)
    t_in = lane >> 4
    colp = powers(abr, abi, CHUNK - 1)
    pr = jnp.where(t_in == CHUNK - 1, 1.0, 0.0) + jnp.zeros_like(bbr)
    pi = jnp.zeros_like(bbr)
    for t in range(CHUNK - 1):
        qr, qi = colp[CHUNK - 2 - t]
        pr = jnp.where(t_in == t, two(qr), pr)
        pi = jnp.where(t_in == t, two(qi), pi)
    wzr = pr * bbr - pi * bbi
    wzi = pr * bbi + pi * bbr
    wz_ref[0:p, :] = wzr.astype(BF16)
    wz_ref[p:2 * p, :] = wzi.astype(BF16)

    krev = (jnp.dot(cre_ref[...], wzr, precision=lax.Precision.HIGHEST, preferred_element_type=F32)
            - jnp.dot(cim_ref[...], wzi, precision=lax.Precision.HIGHEST, preferred_element_type=F32))
    for t in range(CHUNK):
        s = (CHUNK - 1 - t) * S5_GROUP
        blk = krev if s == 0 else pltpu.roll(krev, CHUNK_W - s, 1)
        blk = jnp.where(lane < CHUNK_W - s, blk, 0.0)
        m_ref[t * S5_GROUP:(t + 1) * S5_GROUP, :] = blk.astype(BF16)

    lamr_row = lrr_ref[0:1, :] * dt
    lami_row = lir_ref[0:1, :] * dt
    mag_row = jnp.exp(lamr_row)
    rowp = powers(mag_row * jnp.cos(lami_row), mag_row * jnp.sin(lami_row), CHUNK)
    pwr = jnp.concatenate([jnp.broadcast_to(q[0], (S5_GROUP, LANE)) for q in rowp], axis=0)
    pwi = jnp.concatenate([jnp.broadcast_to(q[1], (S5_GROUP, LANE)) for q in rowp], axis=0)
    cr2 = crt_ref[...]
    ci2 = cit_ref[...]
    lane128 = lax.broadcasted_iota(I32, (1, LANE), 1)
    sel = (lane128 >> 6) == (g % 2)
    wyre_ref[...] = jnp.where(sel, cr2 * pwr - ci2 * pwi, 0.0).astype(BF16)
    wyim_ref[...] = jnp.where(sel, -(cr2 * pwi) - ci2 * pwr, 0.0).astype(BF16)

    a16re_ref[...] = jnp.broadcast_to(rowp[CHUNK - 1][0], (8, LANE))
    a16im_ref[...] = jnp.broadcast_to(rowp[CHUNK - 1][1], (8, LANE))


def _s5_prep(a_re, a_im, log_dt, b_re, b_im, c_re, c_im):
    g, p = a_re.shape
    lrc = jnp.broadcast_to(a_re[:, :, None], (g, p, LANE))
    lic = jnp.broadcast_to(a_im[:, :, None], (g, p, LANE))
    ldt = jnp.broadcast_to(log_dt[:, None, None], (g, 8, LANE))
    brt = jnp.tile(b_re, (1, 1, CHUNK))
    bit = jnp.tile(b_im, (1, 1, CHUNK))
    lrr = jnp.broadcast_to(jnp.concatenate([a_re, a_re], axis=1)[:, None, :], (g, 8, LANE))
    lir = jnp.broadcast_to(jnp.concatenate([a_im, a_im], axis=1)[:, None, :], (g, 8, LANE))
    crt = jnp.tile(jnp.concatenate([c_re, c_re], axis=2), (1, CHUNK, 1))
    cit = jnp.tile(jnp.concatenate([c_im, c_im], axis=2), (1, CHUNK, 1))

    def spec(r, c):
        return pl.BlockSpec((None, r, c), lambda i: (i, 0, 0))

    return pl.pallas_call(
        _s5_prep_kernel,
        grid=(g,),
        in_specs=[spec(p, LANE), spec(p, LANE), spec(8, LANE), spec(p, CHUNK_W), spec(p, CHUNK_W),
                  spec(S5_GROUP, p), spec(S5_GROUP, p), spec(8, LANE), spec(8, LANE),
                  spec(CHUNK_W, LANE), spec(CHUNK_W, LANE)],
        out_specs=[spec(CHUNK_W, CHUNK_W), spec(2 * p, CHUNK_W), spec(CHUNK_W, LANE),
                   spec(CHUNK_W, LANE), spec(8, LANE), spec(8, LANE)],
        out_shape=[jax.ShapeDtypeStruct((g, CHUNK_W, CHUNK_W), BF16),
                   jax.ShapeDtypeStruct((g, 2 * p, CHUNK_W), BF16),
                   jax.ShapeDtypeStruct((g, CHUNK_W, LANE), BF16),
                   jax.ShapeDtypeStruct((g, CHUNK_W, LANE), BF16),
                   jax.ShapeDtypeStruct((g, 8, LANE), F32),
                   jax.ShapeDtypeStruct((g, 8, LANE), F32)],
        compiler_params=_cparams("parallel"),
        name="s5_prep",
    )(lrc, lic, ldt, brt, bit, c_re, c_im, lrr, lir, crt, cit)


def _gelu_tanh(x):
    return x * (0.5 * (1.0 + jnp.tanh(0.7978845608028654 * (x + 0.044715 * (x * x * x)))))


def _s5_core_kernel(ut_ref, m_ref, wz_ref, wyre_ref, wyim_ref, are_ref, aim_ref, d_ref,
                    o_ref, zre_s, zim_s, spre_s, spim_s, *, nb, nc):
    p = S5_STATE
    slp = ut_ref.shape[2]
    vs, zs = [], []
    for q in range(2):
        v = ut_ref[:, q * S5_GROUP:(q + 1) * S5_GROUP, :].reshape(CHUNK_W, slp)
        vs.append(v)
        zs.append(jnp.dot(wz_ref[q], v, preferred_element_type=F32))
    zre_s[...] = jnp.concatenate([zs[0][0:p], zs[1][0:p]], axis=0).T
    zim_s[...] = jnp.concatenate([zs[0][p:2 * p], zs[1][p:2 * p]], axis=0).T

    lane = lax.broadcasted_iota(I32, (1, LANE), 1)
    are = jnp.where(lane < p, are_ref[0, 0:1, :], are_ref[1, 0:1, :])
    aim = jnp.where(lane < p, aim_ref[0, 0:1, :], aim_ref[1, 0:1, :])

    def step(c, carry):
        sr, si = carry
        r0 = pl.multiple_of(c * nb, nb)
        spre_s[pl.ds(r0, nb), :] = sr.astype(BF16)
        spim_s[pl.ds(r0, nb), :] = si.astype(BF16)
        zr = zre_s[pl.ds(r0, nb), :]
        zi = zim_s[pl.ds(r0, nb), :]
        return are * sr - aim * si + zr, are * si + aim * sr + zi

    rm = nc * nb
    spre_s[rm:slp, :] = jnp.zeros((slp - rm, LANE), BF16)
    spim_s[rm:slp, :] = jnp.zeros((slp - rm, LANE), BF16)
    lax.fori_loop(0, nc, step, (zre_s[rm:rm + nb, :], zim_s[rm:rm + nb, :]))

    spre = spre_s[...]
    spim = spim_s[...]
    for q in range(2):
        y_in = jnp.dot(m_ref[q], vs[q], preferred_element_type=F32)
        y_st = (lax.dot_general(wyre_ref[q], spre, _NT, preferred_element_type=F32)
                + lax.dot_general(wyim_ref[q], spim, _NT, preferred_element_type=F32))
        d = d_ref[q]
        for jb in range(slp // LANE):
            sl = slice(jb * LANE, (jb + 1) * LANE)
            y = y_in[:, sl] + y_st[:, sl] + d * vs[q][:, sl].astype(F32)
            z = _gelu_tanh(y).astype(BF16)
            o_ref[:, q * S5_GROUP:(q + 1) * S5_GROUP, sl] = z.reshape(CHUNK, S5_GROUP, LANE)


def _s5_core(ut, ops, d_b, nb, nc):
    m, wz, wyre, wyim, a16re, a16im = ops
    _, f, slp = ut.shape
    g = f // S5_GROUP

    def pair(r, c):
        return pl.BlockSpec((2, r, c), lambda i: (i, 0, 0))

    io_spec = pl.BlockSpec((CHUNK, 2 * S5_GROUP, slp), lambda i: (0, i, 0))
    return pl.pallas_call(
        functools.partial(_s5_core_kernel, nb=nb, nc=nc),
        grid=(g // 2,),
        in_specs=[io_spec, pair(CHUNK_W, CHUNK_W), pair(2 * S5_STATE, CHUNK_W),
                  pair(CHUNK_W, LANE), pair(CHUNK_W, LANE), pair(8, LANE), pair(8, LANE),
                  pair(CHUNK_W, LANE)],
        out_specs=io_spec,
        out_shape=jax.ShapeDtypeStruct(ut.shape, BF16),
        scratch_shapes=[pltpu.VMEM((slp, LANE), F32), pltpu.VMEM((slp, LANE), F32),
                        pltpu.VMEM((slp, LANE), BF16), pltpu.VMEM((slp, LANE), BF16)],
        compiler_params=_cparams("parallel"),
        name="s5_core",
    )(ut, m, wz, wyre, wyim, a16re, a16im, d_b)


def _s5_out_kernel(z_ref, wv_ref, wg_ref, h_ref, o_ref, zt_s):
    @pl.when(pl.program_id(1) == 0)
    def _():
        zt_s[...] = z_ref[...].T

    z = zt_s[...]
    v = jnp.dot(z, wv_ref[...], preferred_element_type=F32)
    gate = jnp.dot(z, wg_ref[...], preferred_element_type=F32)
    o_ref[...] = h_ref[...] + v * jax.nn.sigmoid(gate)


def _s5_out(zt, w_glu, h, slp):
    _, f, _ = zt.shape
    n, d = h.shape
    tn = min(256, d)
    nj = d // tn
    h3 = h.reshape(CHUNK, slp, d)
    out = pl.pallas_call(
        _s5_out_kernel,
        grid=(CHUNK, nj),
        in_specs=[pl.BlockSpec((None, f, slp), lambda t, j: (t, 0, 0)),
                  pl.BlockSpec((f, tn), lambda t, j: (0, j)),
                  pl.BlockSpec((f, tn), lambda t, j: (0, nj + j)),
                  pl.BlockSpec((None, slp, tn), lambda t, j: (t, 0, j))],
        out_specs=pl.BlockSpec((None, slp, tn), lambda t, j: (t, 0, j)),
        out_shape=jax.ShapeDtypeStruct(h3.shape, F32),
        scratch_shapes=[pltpu.VMEM((slp, f), BF16)],
        compiler_params=_cparams("parallel", "arbitrary"),
        name="s5_out",
    )(zt, w_glu, w_glu, h3)
    return out.reshape(n, d)


def _pool_body(win, first, u_ref, halo_ref, w_ref, sc_ref, h_ref, o_ref, nb):
    rows = u_ref.shape[1]
    row = lax.broadcasted_iota(I32, (rows, 1), 0)

    def cur(t):
        return u_ref[t]

    def prev(t):
        halo = jnp.where(first, 0.0, halo_ref[t])
        return jnp.concatenate([halo, u_ref[t, 0:rows - nb, :]], axis=0)

    def at(tau):
        return cur(tau) if tau >= 0 else prev(tau + CHUNK)

    run = at(0)
    for j in range(1, win):
        run = run + at(-j)
    wmat = w_ref[...]
    scale = sc_ref[...]
    for t in range(CHUNK):
        if t > 0:
            run = run + at(t) - at(t - win)
        if t + 1 >= win:
            inv = 1.0 / win
        else:
            inv = jnp.where(jnp.logical_and(first, row < nb), 1.0 / (t + 1), 1.0 / win)
        mixed = run * inv - cur(t)
        y = jnp.dot(mixed.astype(BF16), wmat, preferred_element_type=F32)
        o_ref[t] = h_ref[t] + y * scale


def _pool_kernel(u_ref, halo_ref, w_ref, sc_ref, h_ref, o_ref, *, nb, n_real):
    g = pl.program_id(0)
    first = pl.program_id(1) == n_real
    for gi, win in enumerate(POOL_WINDOWS):
        @pl.when(g == gi)
        def _(win=win):
            _pool_body(win, first, u_ref, halo_ref, w_ref, sc_ref, h_ref, o_ref, nb)


def _pool_core(u, w_grp, scale, h, nb, slp, nc):
    n, d = h.shape
    ng = len(POOL_WINDOWS)
    cw = d // ng
    rt = LANE
    per = rt // nb
    u3 = u.reshape(CHUNK, slp, d)
    h3 = h.reshape(CHUNK, slp, d)
    blk = pl.BlockSpec((CHUNK, rt, cw), lambda g, i: (0, i, g))

    def halo_map(g, i):
        return (0, jnp.where(i == 0, nc, jnp.maximum(i * per - 1, 0)), g)

    out = pl.pallas_call(
        functools.partial(_pool_kernel, nb=nb, n_real=nc // per),
        grid=(ng, slp // rt),
        in_specs=[blk,
                  pl.BlockSpec((CHUNK, nb, cw), halo_map),
                  pl.BlockSpec((None, cw, cw), lambda g, i: (g, 0, 0)),
                  pl.BlockSpec((1, cw), lambda g, i: (0, g)),
                  blk],
        out_specs=blk,
        out_shape=jax.ShapeDtypeStruct(h3.shape, F32),
        compiler_params=_cparams("parallel", "parallel"),
        name="pool_core",
    )(u3, u3, w_grp, scale.reshape(1, d), h3)
    return out.reshape(n, d)


def _router_kernel(h_ref, g_ref, whi_ref, wlo_ref, b_ref, eid_ref, wts_ref, rank_ref, cnt_ref, carry_s):
    i = pl.program_id(0)
    tm = h_ref.shape[0]

    @pl.when(i == 0)
    def _():
        carry_s[...] = jnp.zeros_like(carry_s)

    xn = _rms(h_ref[...], g_ref[...])
    x_hi = xn.astype(BF16)
    x_lo = (xn - x_hi.astype(F32)).astype(BF16)
    w_hi = whi_ref[...]
    lt = (jnp.dot(x_hi, w_hi, preferred_element_type=F32)
          + jnp.dot(x_lo, w_hi, preferred_element_type=F32)
          + jnp.dot(x_hi, wlo_ref[...], preferred_element_type=F32))
    logits = lt.T + b_ref[:, 0:1]

    c = [logits[r:r + 1] for r in range(N_EXPERT_GROUPS)]
    cmax = jnp.maximum(jnp.maximum(c[0], c[1]), jnp.maximum(c[2], c[3]))
    grp = jnp.where(c[0] == cmax, 0, jnp.where(c[1] == cmax, 1, jnp.where(c[2] == cmax, 2, 3)))
    csum = (jnp.exp(c[0] - cmax) + jnp.exp(c[1] - cmax)) + (jnp.exp(c[2] - cmax) + jnp.exp(c[3] - cmax))
    p_grp = 1.0 / csum

    epg = EXPERTS_PER_GROUP
    sel = jnp.zeros((epg, tm), F32)
    for gi in range(N_EXPERT_GROUPS):
        blk = logits[FINE_ROW0 + gi * epg:FINE_ROW0 + (gi + 1) * epg]
        sel = jnp.where(grp == gi, blk, sel)
    ridx = lax.broadcasted_iota(I32, (epg, tm), 0)
    m1 = jnp.max(sel, axis=0, keepdims=True)
    i1 = jnp.min(jnp.where(sel == m1, ridx, epg), axis=0, keepdims=True)
    sel2 = jnp.where(ridx == i1, -jnp.inf, sel)
    m2 = jnp.max(sel2, axis=0, keepdims=True)
    i2 = jnp.min(jnp.where(sel2 == m2, ridx, epg), axis=0, keepdims=True)
    ssum = jnp.sum(jnp.exp(sel - m1), axis=0, keepdims=True)
    p1 = 1.0 / ssum
    p2 = jnp.exp(m2 - m1) / ssum
    den = p1 + p2
    e1 = grp * epg + i1
    e2 = grp * epg + i2
    eid_ref[0:1, :] = e1
    eid_ref[1:2, :] = e2
    wts_ref[0:1, :] = p1 / den * p_grp
    wts_ref[1:2, :] = p2 / den * p_grp

    eidx = lax.broadcasted_iota(I32, (N_EXPERTS, tm), 0)
    hit1 = eidx == e1
    hit2 = eidx == e2
    onehot = jnp.where(hit1, 1.0, jnp.where(hit2, 1.0, 0.0))
    tri = jnp.where(lax.broadcasted_iota(I32, (tm, tm), 0) <= lax.broadcasted_iota(I32, (tm, tm), 1),
                    1.0, 0.0).astype(BF16)
    pref = jnp.dot(onehot.astype(BF16), tri, preferred_element_type=F32)
    tot = pref + carry_s[:, 0:1]
    rank_ref[0:1, :] = (jnp.sum(jnp.where(hit1, tot, 0.0), axis=0, keepdims=True) - 1.0).astype(I32)
    rank_ref[1:2, :] = (jnp.sum(jnp.where(hit2, tot, 0.0), axis=0, keepdims=True) - 1.0).astype(I32)
    carry_s[...] = carry_s[...] + pref[:, tm - 1:tm]
    cnt_ref[...] = carry_s[...]


def _router(h, g, w_hi, w_lo, bias):
    n, d = h.shape
    tm = TOK_TILE
    two = pl.BlockSpec((2, tm), lambda i: (0, i))
    return pl.pallas_call(
        _router_kernel,
        grid=(n // tm,),
        in_specs=[pl.BlockSpec((tm, d), lambda i: (i, 0)),
                  pl.BlockSpec((1, d), lambda i: (0, 0)),
                  pl.BlockSpec((d, ROUTER_ROWS), lambda i: (0, 0)),
                  pl.BlockSpec((d, ROUTER_ROWS), lambda i: (0, 0)),
                  pl.BlockSpec((ROUTER_ROWS, LANE), lambda i: (0, 0))],
        out_specs=[two, two, two, pl.BlockSpec((N_EXPERTS, LANE), lambda i: (0, 0))],
        out_shape=[jax.ShapeDtypeStruct((2, n), I32), jax.ShapeDtypeStruct((2, n), F32),
                   jax.ShapeDtypeStruct((2, n), I32), jax.ShapeDtypeStruct((N_EXPERTS, LANE), F32)],
        scratch_shapes=[pltpu.VMEM((N_EXPERTS, LANE), F32)],
        compiler_params=_cparams("arbitrary"),
        name="moe_router",
    )(h, g.reshape(1, d), w_hi, w_lo, bias)


def _router_params(w_coarse, b_coarse, w_fine, b_fine):
    d = w_coarse.shape[0]
    pad0 = jnp.zeros((d, FINE_ROW0 - N_EXPERT_GROUPS), F32)
    pad1 = jnp.zeros((d, ROUTER_ROWS - FINE_ROW0 - N_EXPERTS), F32)
    wr = jnp.concatenate([w_coarse.astype(F32), pad0, w_fine.astype(F32), pad1], axis=1)
    w_hi = wr.astype(BF16)
    w_lo = (wr - w_hi.astype(F32)).astype(BF16)
    bias = jnp.concatenate([b_coarse.astype(F32), pad0[0], b_fine.astype(F32), pad1[0]])
    return w_hi, w_lo, jnp.broadcast_to(bias[:, None], (ROUTER_ROWS, LANE))


def _route_tables(counts, n_tiles):
    cnt = counts[:, 0].astype(I32)
    tiles_e = (cnt + EXP_TILE - 1) // EXP_TILE
    tile_end = jnp.cumsum(tiles_e)
    tile_start = tile_end - tiles_e
    n_used = tile_end[-1]
    tq = jnp.minimum(jnp.arange(n_tiles, dtype=I32), n_used - 1)
    te = jnp.sum((tile_end[None, :] <= tq[:, None]).astype(I32), axis=1)
    onehot = (te[:, None] == jnp.arange(N_EXPERTS, dtype=I32)[None, :]).astype(I32)
    cnt_t = jnp.sum(onehot * cnt[None, :], axis=1)
    start_t = jnp.sum(onehot * tile_start[None, :], axis=1)
    tid = jnp.arange(n_tiles, dtype=I32)
    valid = jnp.where(tid < n_used, jnp.clip(cnt_t - (tid - start_t) * EXP_TILE, 0, EXP_TILE), 0)
    seg_start = jnp.broadcast_to((tile_start * EXP_TILE)[:, None], (N_EXPERTS, LANE))
    pad_start = tile_start * EXP_TILE + cnt
    return (seg_start.astype(I32), pad_start.astype(I32), te.astype(I32), valid.astype(I32),
            jnp.reshape(n_used, (1,)).astype(I32))


def _pos_kernel(eid_ref, rank_ref, st_ref, pos_ref):
    tm = eid_ref.shape[1]
    eidx = lax.broadcasted_iota(I32, (N_EXPERTS, tm), 0)
    st = st_ref[:, 0:1]
    for k in range(2):
        base = jnp.sum(jnp.where(eidx == eid_ref[k:k + 1, :], st, 0), axis=0, keepdims=True)
        pos_ref[k:k + 1, :] = base + rank_ref[k:k + 1, :]


def _positions(eid, rank, seg_start):
    n = eid.shape[1]
    tm = TOK_TILE
    two = pl.BlockSpec((2, tm), lambda i: (0, i))
    return pl.pallas_call(
        _pos_kernel,
        grid=(n // tm,),
        in_specs=[two, two, pl.BlockSpec((N_EXPERTS, LANE), lambda i: (0, 0))],
        out_specs=pl.BlockSpec((None, 2, tm), lambda i: (i, 0, 0)),
        out_shape=jax.ShapeDtypeStruct((n // tm, 2, tm), I32),
        compiler_params=_cparams("parallel"),
        name="moe_positions",
    )(eid, rank, seg_start)


def _store_row_tiles(dst, words):
    for j in range(dst.shape[1]):
        dst[:, j, :] = words[:, j * LANE:(j + 1) * LANE]


def _load_row_tiles(src):
    return jnp.concatenate([src[:, j, :] for j in range(src.shape[1])], axis=1)


def _dispatch_kernel(pad_ref, nu_ref, pos_ref, h_ref, g_ref, xs_ref, buf, zbuf, sem, zsem):
    i = pl.program_id(0)
    n = pl.num_programs(0)
    tm = h_ref.shape[0]
    half = h_ref.shape[1] // 2
    slot = i % 2

    def wait_slot(s):
        for _ in range(2):
            pltpu.make_async_copy(buf.at[s], xs_ref.at[pl.ds(0, tm)], sem.at[s]).wait()

    @pl.when(i == 0)
    def _():
        zbuf[...] = jnp.zeros_like(zbuf)
        zt = zbuf.shape[0]
        sizes = [zt >> (k + 1) for k in range(zt.bit_length() - 1)]

        def pad_pieces(e, act):
            off = pad_ref[e]
            length = (zt - (off & (zt - 1))) & (zt - 1)
            for sz in sizes:
                take = (length & sz) != 0

                @pl.when(take)
                def _(off=off, sz=sz):
                    act(pltpu.make_async_copy(zbuf.at[pl.ds(0, sz)], xs_ref.at[pl.ds(off, sz)], zsem))

                off = off + jnp.where(take, sz, 0)

        def tail_piece(j, act):
            act(pltpu.make_async_copy(zbuf, xs_ref.at[pl.ds(pl.multiple_of(j * zt, zt), zt)], zsem))

        n_all = xs_ref.shape[0] // zt
        for act in (lambda c: c.start(), lambda c: c.wait()):
            lax.fori_loop(0, N_EXPERTS, lambda e, carry, act=act: (pad_pieces(e, act), carry)[1], 0)
            lax.fori_loop(nu_ref[0], n_all, lambda j, carry, act=act: (tail_piece(j, act), carry)[1], 0)

    @pl.when(i >= 2)
    def _():
        wait_slot(slot)

    xn = _rms(h_ref[...], g_ref[...])
    _store_row_tiles(buf.at[slot], _pack_pairs(xn[:, :half], xn[:, half:]))

    def issue(r, carry):
        for k in range(2):
            pltpu.make_async_copy(buf.at[slot, r], xs_ref.at[pos_ref[k, r]], sem.at[slot]).start(priority=k)
        return carry

    lax.fori_loop(0, tm, issue, 0, unroll=8)

    @pl.when(i == n - 1)
    def _():
        wait_slot(slot)

        @pl.when(n >= 2)
        def _():
            wait_slot(1 - slot)


def _dispatch(h, g, pos3, pad_start, n_used, n_rows, word_dtype):
    n, d = h.shape
    tm = TOK_TILE
    nl = d // 2 // LANE
    grid_spec = pltpu.PrefetchScalarGridSpec(
        num_scalar_prefetch=2,
        grid=(n // tm,),
        in_specs=[pl.BlockSpec((None, 2, tm), lambda i, pad, nu: (i, 0, 0), memory_space=pltpu.SMEM),
                  pl.BlockSpec((tm, d), lambda i, pad, nu: (i, 0)),
                  pl.BlockSpec((1, d), lambda i, pad, nu: (0, 0))],
        out_specs=pl.BlockSpec(memory_space=pl.ANY),
        scratch_shapes=[pltpu.VMEM((2, tm, nl, LANE), word_dtype),
                        pltpu.VMEM((EXP_TILE, nl, LANE), word_dtype),
                        pltpu.SemaphoreType.DMA((2,)), pltpu.SemaphoreType.DMA(())],
    )
    return pl.pallas_call(
        _dispatch_kernel,
        grid_spec=grid_spec,
        out_shape=jax.ShapeDtypeStruct((n_rows, nl, LANE), word_dtype),
        compiler_params=_cparams("arbitrary"),
        name="moe_dispatch",
    )(pad_start, n_used, pos3, h, g.reshape(1, d))


def _expert_kernel(te_ref, va_ref, nu_ref, xs_ref, wg_ref, wu_ref, wd_ref, ys_ref, wgb, wub, wdb):
    del nu_ref
    i = pl.program_id(0)
    changed = jnp.logical_or(i == 0, te_ref[i] != te_ref[jnp.maximum(i - 1, 0)])

    @pl.when(changed)
    def _():
        wgb[...] = wg_ref[...].astype(BF16)
        wub[...] = wu_ref[...].astype(BF16)
        wdb[...] = wd_ref[...].astype(BF16)

    valid = va_ref[i]

    @pl.when(valid > 0)
    def _():
        lo, hi = _unpack_pairs(_load_row_tiles(xs_ref))
        lo = lo.astype(BF16)
        hi = hi.astype(BF16)
        half = lo.shape[1]
        gate = (jnp.dot(lo, wgb[0:half, :], preferred_element_type=F32)
                + jnp.dot(hi, wgb[half:2 * half, :], preferred_element_type=F32))
        up = (jnp.dot(lo, wub[0:half, :], preferred_element_type=F32)
              + jnp.dot(hi, wub[half:2 * half, :], preferred_element_type=F32))
        hid = (gate * jax.nn.sigmoid(gate)) * up
        y = jnp.dot(hid.astype(BF16), wdb[...], preferred_element_type=F32)
        _store_row_tiles(ys_ref, _pack_pairs(y[:, :half], y[:, half:]))

    @pl.when(valid <= 0)
    def _():
        ys_ref[...] = jnp.zeros_like(ys_ref)


def _experts(xs, te, valid, n_used, w_gate, w_up, w_down):
    n_tiles = te.shape[0]
    _, nl, _ = xs.shape
    _, d, de = w_gate.shape
    te_tile = EXP_TILE
    grid_spec = pltpu.PrefetchScalarGridSpec(
        num_scalar_prefetch=3,
        grid=(n_tiles,),
        in_specs=[pl.BlockSpec((te_tile, nl, LANE), lambda i, te, va, nu: (jnp.minimum(i, nu[0] - 1), 0, 0)),
                  pl.BlockSpec((None, d, de), lambda i, te, va, nu: (te[i], 0, 0)),
                  pl.BlockSpec((None, d, de), lambda i, te, va, nu: (te[i], 0, 0)),
                  pl.BlockSpec((None, de, d), lambda i, te, va, nu: (te[i], 0, 0))],
        out_specs=pl.BlockSpec((te_tile, nl, LANE), lambda i, te, va, nu: (i, 0, 0)),
        scratch_shapes=[pltpu.VMEM((d, de), BF16), pltpu.VMEM((d, de), BF16), pltpu.VMEM((de, d), BF16)],
    )
    return pl.pallas_call(
        _expert_kernel,
        grid_spec=grid_spec,
        out_shape=jax.ShapeDtypeStruct((n_tiles * te_tile, nl, LANE), xs.dtype),
        compiler_params=_cparams("arbitrary"),
        name="moe_experts",
    )(te, valid, n_used, xs, w_gate, w_up, w_down)


def _combine_kernel(pos_ref, posn_ref, w_ref, h_ref, *rest, with_norm):
    if with_norm:
        g_ref, ys_ref, o_ref, hn_ref, buf, sem = rest
    else:
        ys_ref, o_ref, buf, sem = rest
    i = pl.program_id(0)
    n = pl.num_programs(0)
    tm = h_ref.shape[0]
    half = h_ref.shape[1] // 2
    slot = i % 2

    def issue(p_ref, s):
        def body(r, carry):
            for k in range(2):
                pltpu.make_async_copy(ys_ref.at[p_ref[k, r]], buf.at[s, k, r], sem.at[s]).start(priority=k)
            return carry
        lax.fori_loop(0, tm, body, 0, unroll=8)

    @pl.when(i == 0)
    def _():
        issue(pos_ref, 0)

    @pl.when(i + 1 < n)
    def _():
        issue(posn_ref, 1 - slot)

    for k in range(2):
        pltpu.make_async_copy(ys_ref.at[pl.ds(0, tm)], buf.at[slot, k], sem.at[slot]).wait()

    lo0, hi0 = _unpack_pairs(_load_row_tiles(buf.at[slot, 0]))
    lo1, hi1 = _unpack_pairs(_load_row_tiles(buf.at[slot, 1]))
    w0 = jnp.broadcast_to(w_ref[0:1, :], (LANE, tm)).T[:, 0:1]
    w1 = jnp.broadcast_to(w_ref[1:2, :], (LANE, tm)).T[:, 0:1]
    h = h_ref[...]
    hnew = jnp.concatenate([h[:, :half] + (w0 * lo0 + w1 * lo1),
                            h[:, half:] + (w0 * hi0 + w1 * hi1)], axis=1)
    o_ref[...] = hnew
    if with_norm:
        hn_ref[...] = _rms(hnew, g_ref[...]).astype(hn_ref.dtype)


def _combine(h, ys, pos3, wts, g_next):
    n, d = h.shape
    tm = TOK_TILE
    nt = n // tm
    nl = ys.shape[1]
    with_norm = g_next is not None
    smem = functools.partial(pl.BlockSpec, (None, 2, tm), memory_space=pltpu.SMEM)
    row = pl.BlockSpec((tm, d), lambda i: (i, 0))
    in_specs = [smem(lambda i: (i, 0, 0)),
                smem(lambda i: (jnp.minimum(i + 1, nt - 1), 0, 0)),
                pl.BlockSpec((2, tm), lambda i: (0, i)),
                row]
    args = [pos3, pos3, wts, h]
    out_specs = [row]
    out_shape = [jax.ShapeDtypeStruct((n, d), F32)]
    if with_norm:
        in_specs.append(pl.BlockSpec((1, d), lambda i: (0, 0)))
        args.append(g_next.reshape(1, d))
        out_specs.append(row)
        out_shape.append(jax.ShapeDtypeStruct((n, d), BF16))
    in_specs.append(pl.BlockSpec(memory_space=pl.ANY))
    args.append(ys)
    out = pl.pallas_call(
        functools.partial(_combine_kernel, with_norm=with_norm),
        grid=(nt,),
        in_specs=in_specs,
        out_specs=out_specs,
        out_shape=out_shape,
        scratch_shapes=[pltpu.VMEM((2, 2, tm, nl, LANE), ys.dtype), pltpu.SemaphoreType.DMA((2,))],
        compiler_params=_cparams("arbitrary"),
        name="moe_combine",
    )(*args)
    return (out[0], out[1]) if with_norm else (out[0], None)


def _hier_moe(h, g_ffn, g_next, w_coarse, b_coarse, w_fine, b_fine, w_gate, w_up, w_down):
    n = h.shape[0]
    n_tiles = (2 * n + N_EXPERTS * (EXP_TILE - 1) + EXP_TILE - 1) // EXP_TILE
    w_hi, w_lo, bias = _router_params(w_coarse, b_coarse, w_fine, b_fine)
    eid, wts, rank, counts = _router(h, g_ffn, w_hi, w_lo, bias)
    seg_start, pad_start, te, valid, n_used = _route_tables(counts, n_tiles)
    pos3 = _positions(eid, rank, seg_start)
    word_dtype = jax.eval_shape(lambda a: _pack_pairs(a, a), jax.ShapeDtypeStruct((8, LANE), F32)).dtype
    xs = _dispatch(h, g_ffn, pos3, pad_start, n_used, n_tiles * EXP_TILE, word_dtype)
    ys = _experts(xs, te, valid, n_used, w_gate, w_up, w_down)
    return _combine(h, ys, pos3, wts, g_next)


def kernel(x, meta_tokens, norm_mix, norm_ffn, norm_final, s5_w_in, s5_a_re, s5_a_im, s5_log_dt,
           s5_b_re, s5_b_im, s5_c_re, s5_c_im, s5_d, s5_w_glu, pool_w_in, pool_w_grp, pool_scale,
           moe_w_coarse, moe_b_coarse, moe_w_fine, moe_b_fine, moe_w_gate, moe_w_up, moe_w_down):
    b, seq, d = x.shape
    depth = norm_mix.shape[0]
    assert meta_tokens.shape[0] == N_META == CHUNK and seq % CHUNK == 0
    assert LANE % b == 0 and b % 16 == 0
    per = LANE // b
    nc = seq // CHUNK
    assert nc % per == 0
    ncp = nc + per
    slp = ncp * b

    h, hn = _ingest(x, meta_tokens, norm_mix[0], ncp)
    for i in range(depth):
        j = i // 2
        if i % 2 == 0:
            g = s5_a_re.shape[1]
            ops = _s5_prep(s5_a_re[j], s5_a_im[j], s5_log_dt[j], s5_b_re[j], s5_b_im[j],
                           s5_c_re[j], s5_c_im[j])
            d_b = jnp.broadcast_to(jnp.tile(s5_d[j].reshape(g, 1, S5_GROUP), (1, CHUNK, 1))
                                   .reshape(g, CHUNK_W, 1), (g, CHUNK_W, LANE))
            ut = _s5_in(hn, s5_w_in[j].astype(BF16), slp)
            zt = _s5_core(ut, ops, d_b, b, nc)
            h = _s5_out(zt, s5_w_glu[j].astype(BF16), h, slp)
        else:
            u = _matmul(hn, pool_w_in[j].astype(BF16), F32)
            h = _pool_core(u, pool_w_grp[j].astype(BF16), pool_scale[j], h, b, slp, nc)
        g_next = None if i == depth - 1 else norm_mix[i + 1]
        h, hn = _hier_moe(h, norm_ffn[i], g_next,
                          moe_w_coarse[i], moe_b_coarse[i], moe_w_fine[i], moe_b_fine[i],
                          moe_w_gate[i], moe_w_up[i], moe_w_down[i])
    return _egress(h, norm_final, b, seq, ncp)
```

```python
import functools

import jax
import jax.numpy as jnp
from jax import lax
from jax.experimental import pallas as pl
from jax.experimental.pallas import tpu as pltpu

F32 = jnp.float32
BF16 = jnp.bfloat16
I32 = jnp.int32

N_META = 16
CHUNK = 16
RMS_EPS = 1e-6
S5_GROUP = 16
S5_STATE = 64
CHUNK_W = CHUNK * S5_GROUP
POOL_WINDOWS = (2, 4, 8, 16)
N_EXPERT_GROUPS = 4
EXPERTS_PER_GROUP = 8
N_EXPERTS = N_EXPERT_GROUPS * EXPERTS_PER_GROUP
ROUTER_ROWS = 128
FINE_ROW0 = 8

LANE = 128
TOK_TILE = 256
EXP_TILE = 256
VMEM_LIMIT = 56 * 1024 * 1024

_NT = (((1,), (1,)), ((), ()))
_TN = (((0,), (0,)), ((), ()))


def _cparams(*sem):
    return pltpu.CompilerParams(dimension_semantics=sem, vmem_limit_bytes=VMEM_LIMIT)


def _rms(x, g):
    return x * lax.rsqrt(jnp.mean(x * x, axis=-1, keepdims=True) + RMS_EPS) * g


def _pack_pairs(lo, hi):
    return pltpu.pack_elementwise([lo, hi], packed_dtype=BF16)


def _unpack_pairs(w):
    lo = pltpu.unpack_elementwise(w, index=0, packed_dtype=BF16, unpacked_dtype=F32)
    hi = pltpu.unpack_elementwise(w, index=1, packed_dtype=BF16, unpacked_dtype=F32)
    return lo, hi


def _ingest_kernel(x_ref, meta_ref, g_ref, h_ref, hn_ref, *, nc):
    c = pl.program_id(0)
    g = g_ref[...]

    def emit(t, h):
        h_ref[t] = h
        hn_ref[t] = _rms(h, g).astype(hn_ref.dtype)

    @pl.when(c < nc)
    def _():
        for t in range(CHUNK):
            emit(t, x_ref[:, t, :])

    @pl.when(c == nc)
    def _():
        nb, d = h_ref.shape[1], h_ref.shape[2]
        for t in range(CHUNK):
            emit(t, jnp.broadcast_to(meta_ref[t:t + 1, :], (nb, d)))

    @pl.when(c > nc)
    def _():
        h_ref[...] = jnp.zeros_like(h_ref)
        hn_ref[...] = jnp.zeros_like(hn_ref)


def _ingest(x, meta, g, ncp):
    b, seq, d = x.shape
    nc = seq // CHUNK
    out = pl.BlockSpec((CHUNK, b, d), lambda c: (0, c, 0))
    h3, hn3 = pl.pallas_call(
        functools.partial(_ingest_kernel, nc=nc),
        grid=(ncp,),
        in_specs=[pl.BlockSpec((b, None, CHUNK, d), lambda c: (0, jnp.minimum(c, nc - 1), 0, 0)),
                  pl.BlockSpec((N_META, d), lambda c: (0, 0)),
                  pl.BlockSpec((1, d), lambda c: (0, 0))],
        out_specs=[out, out],
        out_shape=[jax.ShapeDtypeStruct((CHUNK, ncp * b, d), F32),
                   jax.ShapeDtypeStruct((CHUNK, ncp * b, d), BF16)],
        compiler_params=_cparams("parallel"),
        name="ingest",
    )(x.reshape(b, nc, CHUNK, d), meta.astype(F32), g.reshape(1, d))
    n = CHUNK * ncp * b
    return h3.reshape(n, d), hn3.reshape(n, d)


def _egress_kernel(h_ref, g_ref, o_ref):
    g = g_ref[...]
    for t in range(CHUNK):
        o_ref[:, t, :] = _rms(h_ref[t], g)


def _egress(h, g, b, seq, ncp):
    n, d = h.shape
    nc = seq // CHUNK
    out = pl.pallas_call(
        _egress_kernel,
        grid=(nc,),
        in_specs=[pl.BlockSpec((CHUNK, b, d), lambda c: (0, c, 0)),
                  pl.BlockSpec((1, d), lambda c: (0, 0))],
        out_specs=pl.BlockSpec((b, None, CHUNK, d), lambda c: (0, c, 0, 0)),
        out_shape=jax.ShapeDtypeStruct((b, nc, CHUNK, d), F32),
        compiler_params=_cparams("parallel"),
        name="egress",
    )(h.reshape(CHUNK, ncp * b, d), g.reshape(1, d))
    return out.reshape(b, seq, d)


def _mm_kernel(x_ref, w_ref, o_ref):
    o_ref[...] = jnp.dot(x_ref[...], w_ref[...], preferred_element_type=F32).astype(o_ref.dtype)


def _matmul(x, w, out_dtype, tm=512):
    n, k = x.shape
    f = w.shape[1]
    return pl.pallas_call(
        _mm_kernel,
        grid=(n // tm,),
        in_specs=[pl.BlockSpec((tm, k), lambda i: (i, 0)),
                  pl.BlockSpec((k, f), lambda i: (0, 0))],
        out_specs=pl.BlockSpec((tm, f), lambda i: (i, 0)),
        out_shape=jax.ShapeDtypeStruct((n, f), out_dtype),
        compiler_params=_cparams("parallel"),
        name="pool_in",
    )(x, w)


def _mm_t_kernel(x_ref, w_ref, o_ref):
    o_ref[...] = jnp.dot(x_ref[...], w_ref[...], preferred_element_type=F32).astype(o_ref.dtype).T


def _s5_in(hn, w, slp):
    n, k = hn.shape
    f = w.shape[1]
    tf = min(512, f)
    return pl.pallas_call(
        _mm_t_kernel,
        grid=(CHUNK, f // tf),
        in_specs=[pl.BlockSpec((None, slp, k), lambda t, j: (t, 0, 0)),
                  pl.BlockSpec((k, tf), lambda t, j: (0, j))],
        out_specs=pl.BlockSpec((None, tf, slp), lambda t, j: (t, j, 0)),
        out_shape=jax.ShapeDtypeStruct((CHUNK, f, slp), BF16),
        compiler_params=_cparams("parallel", "arbitrary"),
        name="s5_in",
    )(hn.reshape(CHUNK, slp, k), w)


def _s5_prep_kernel(lrc_ref, lic_ref, ldt_ref, brt_ref, bit_ref, cre_ref, cim_ref,
                    lrr_ref, lir_ref, crt_ref, cit_ref,
                    m_ref, wz_ref, wyre_ref, wyim_ref, a16re_ref, a16im_ref):
    g = pl.program_id(0)
    p = S5_STATE
    dt = jnp.exp(ldt_ref[0:1, :])
    lr = lrc_ref[...]
    li = lic_ref[...]
    lam_r = lr * dt
    lam_i = li * dt
    mag = jnp.exp(lam_r)
    abr = mag * jnp.cos(lam_i)
    abi = mag * jnp.sin(lam_i)
    den = lr * lr + li * li
    nr = abr - 1.0
    cfr = (nr * lr + abi * li) / den
    cfi = (abi * lr - nr * li) / den

    def two(x):
        return jnp.concatenate([x, x], axis=1)

    def powers(ar, ai, n):
        out = [(ar, ai)]
        for _ in range(n - 1):
            qr, qi = out[-1]
            out.append((qr * ar - qi * ai, qr * ai + qi * ar))
        return out

    cfr2, cfi2 = two(cfr), two(cfi)
    brt = brt_ref[...]
    bit = bit_ref[...]
    bbr = cfr2 * brt - cfi2 * bit
    bbi = cfr2 * bit + cfi2 * brt
    lane = lax.broadcasted_iota(I32, (1, CHUNK_W), 1)
    t_in = lane >> 4
    colp = powers(abr, abi, CHUNK - 1)
    pr = jnp.where(t_in == CHUNK - 1, 1.0, 0.0) + jnp.zeros_like(bbr)
    pi = jnp.zeros_like(bbr)
    for t in range(CHUNK - 1):
        qr, qi = colp[CHUNK - 2 - t]
        pr = jnp.where(t_in == t, two(qr), pr)
        pi = jnp.where(t_in == t, two(qi), pi)
    wzr = pr * bbr - pi * bbi
    wzi = pr * bbi + pi * bbr
    wz_ref[0:p, :] = wzr.astype(BF16)
    wz_ref[p:2 * p, :] = wzi.astype(BF16)

    krev = (jnp.dot(cre_ref[...], wzr, precision=lax.Precision.HIGHEST, preferred_element_type=F32)
            - jnp.dot(cim_ref[...], wzi, precision=lax.Precision.HIGHEST, preferred_element_type=F32))
    for t in range(CHUNK):
        s = (CHUNK - 1 - t) * S5_GROUP
        blk = krev if s == 0 else pltpu.roll(krev, CHUNK_W - s, 1)
        blk = jnp.where(lane < CHUNK_W - s, blk, 0.0)
        m_ref[t * S5_GROUP:(t + 1) * S5_GROUP, :] = blk.astype(BF16)

    lamr_row = lrr_ref[0:1, :] * dt
    lami_row = lir_ref[0:1, :] * dt
    mag_row = jnp.exp(lamr_row)
    rowp = powers(mag_row * jnp.cos(lami_row), mag_row * jnp.sin(lami_row), CHUNK)
    pwr = jnp.concatenate([jnp.broadcast_to(q[0], (S5_GROUP, LANE)) for q in rowp], axis=0)
    pwi = jnp.concatenate([jnp.broadcast_to(q[1], (S5_GROUP, LANE)) for q in rowp], axis=0)
    cr2 = crt_ref[...]
    ci2 = cit_ref[...]
    lane128 = lax.broadcasted_iota(I32, (1, LANE), 1)
    sel = (lane128 >> 6) == (g % 2)
    wyre_ref[...] = jnp.where(sel, cr2 * pwr - ci2 * pwi, 0.0).astype(BF16)
    wyim_ref[...] = jnp.where(sel, -(cr2 * pwi) - ci2 * pwr, 0.0).astype(BF16)

    a16re_ref[...] = jnp.broadcast_to(rowp[CHUNK - 1][0], (8, LANE))
    a16im_ref[...] = jnp.broadcast_to(rowp[CHUNK - 1][1], (8, LANE))


def _s5_prep(a_re, a_im, log_dt, b_re, b_im, c_re, c_im):
    g, p = a_re.shape
    lrc = jnp.broadcast_to(a_re[:, :, None], (g, p, LANE))
    lic = jnp.broadcast_to(a_im[:, :, None], (g, p, LANE))
    ldt = jnp.broadcast_to(log_dt[:, None, None], (g, 8, LANE))
    brt = jnp.tile(b_re, (1, 1, CHUNK))
    bit = jnp.tile(b_im, (1, 1, CHUNK))
    lrr = jnp.broadcast_to(jnp.concatenate([a_re, a_re], axis=1)[:, None, :], (g, 8, LANE))
    lir = jnp.broadcast_to(jnp.concatenate([a_im, a_im], axis=1)[:, None, :], (g, 8, LANE))
    crt = jnp.tile(jnp.concatenate([c_re, c_re], axis=2), (1, CHUNK, 1))
    cit = jnp.tile(jnp.concatenate([c_im, c_im], axis=2), (1, CHUNK, 1))

    def spec(r, c):
        return pl.BlockSpec((None, r, c), lambda i: (i, 0, 0))

    return pl.pallas_call(
        _s5_prep_kernel,
        grid=(g,),
        in_specs=[spec(p, LANE), spec(p, LANE), spec(8, LANE), spec(p, CHUNK_W), spec(p, CHUNK_W),
                  spec(S5_GROUP, p), spec(S5_GROUP, p), spec(8, LANE), spec(8, LANE),
                  spec(CHUNK_W, LANE), spec(CHUNK_W, LANE)],
        out_specs=[spec(CHUNK_W, CHUNK_W), spec(2 * p, CHUNK_W), spec(CHUNK_W, LANE),
                   spec(CHUNK_W, LANE), spec(8, LANE), spec(8, LANE)],
        out_shape=[jax.ShapeDtypeStruct((g, CHUNK_W, CHUNK_W), BF16),
                   jax.ShapeDtypeStruct((g, 2 * p, CHUNK_W), BF16),
                   jax.ShapeDtypeStruct((g, CHUNK_W, LANE), BF16),
                   jax.ShapeDtypeStruct((g, CHUNK_W, LANE), BF16),
                   jax.ShapeDtypeStruct((g, 8, LANE), F32),
                   jax.ShapeDtypeStruct((g, 8, LANE), F32)],
        compiler_params=_cparams("parallel"),
        name="s5_prep",
    )(lrc, lic, ldt, brt, bit, c_re, c_im, lrr, lir, crt, cit)


def _gelu_tanh(x):
    return x * (0.5 * (1.0 + jnp.tanh(0.7978845608028654 * (x + 0.044715 * (x * x * x)))))


def _s5_core_kernel(ut_ref, m_ref, wz_ref, wyre_ref, wyim_ref, are_ref, aim_ref, d_ref,
                    o_ref, zre_s, zim_s, spre_s, spim_s, *, nb, nc):
    p = S5_STATE
    slp = ut_ref.shape[2]
    vs, zs = [], []
    for q in range(2):
        v = ut_ref[:, q * S5_GROUP:(q + 1) * S5_GROUP, :].reshape(CHUNK_W, slp)
        vs.append(v)
        zs.append(jnp.dot(wz_ref[q], v, preferred_element_type=F32))
    zre_s[...] = jnp.concatenate([zs[0][0:p], zs[1][0:p]], axis=0).T
    zim_s[...] = jnp.concatenate([zs[0][p:2 * p], zs[1][p:2 * p]], axis=0).T

    lane = lax.broadcasted_iota(I32, (1, LANE), 1)
    are = jnp.where(lane < p, are_ref[0, 0:1, :], are_ref[1, 0:1, :])
    aim = jnp.where(lane < p, aim_ref[0, 0:1, :], aim_ref[1, 0:1, :])

    def step(c, carry):
        sr, si = carry
        r0 = pl.multiple_of(c * nb, nb)
        spre_s[pl.ds(r0, nb), :] = sr.astype(BF16)
        spim_s[pl.ds(r0, nb), :] = si.astype(BF16)
        zr = zre_s[pl.ds(r0, nb), :]
        zi = zim_s[pl.ds(r0, nb), :]
        return are * sr - aim * si + zr, are * si + aim * sr + zi

    rm = nc * nb
    spre_s[rm:slp, :] = jnp.zeros((slp - rm, LANE), BF16)
    spim_s[rm:slp, :] = jnp.zeros((slp - rm, LANE), BF16)
    lax.fori_loop(0, nc, step, (zre_s[rm:rm + nb, :], zim_s[rm:rm + nb, :]))

    spre = spre_s[...]
    spim = spim_s[...]
    for q in range(2):
        y_in = jnp.dot(m_ref[q], vs[q], preferred_element_type=F32)
        y_st = (lax.dot_general(wyre_ref[q], spre, _NT, preferred_element_type=F32)
                + lax.dot_general(wyim_ref[q], spim, _NT, preferred_element_type=F32))
        d = d_ref[q]
        for jb in range(slp // LANE):
            sl = slice(jb * LANE, (jb + 1) * LANE)
            y = y_in[:, sl] + y_st[:, sl] + d * vs[q][:, sl].astype(F32)
            z = _gelu_tanh(y).astype(BF16)
            o_ref[:, q * S5_GROUP:(q + 1) * S5_GROUP, sl] = z.reshape(CHUNK, S5_GROUP, LANE)


def _s5_core(ut, ops, d_b, nb, nc):
    m, wz, wyre, wyim, a16re, a16im = ops
    _, f, slp = ut.shape
    g = f // S5_GROUP

    def pair(r, c):
        return pl.BlockSpec((2, r, c), lambda i: (i, 0, 0))

    io_spec = pl.BlockSpec((CHUNK, 2 * S5_GROUP, slp), lambda i: (0, i, 0))
    return pl.pallas_call(
        functools.partial(_s5_core_kernel, nb=nb, nc=nc),
        grid=(g // 2,),
        in_specs=[io_spec, pair(CHUNK_W, CHUNK_W), pair(2 * S5_STATE, CHUNK_W),
                  pair(CHUNK_W, LANE), pair(CHUNK_W, LANE), pair(8, LANE), pair(8, LANE),
                  pair(CHUNK_W, LANE)],
        out_specs=io_spec,
        out_shape=jax.ShapeDtypeStruct(ut.shape, BF16),
        scratch_shapes=[pltpu.VMEM((slp, LANE), F32), pltpu.VMEM((slp, LANE), F32),
                        pltpu.VMEM((slp, LANE), BF16), pltpu.VMEM((slp, LANE), BF16)],
        compiler_params=_cparams("parallel"),
        name="s5_core",
    )(ut, m, wz, wyre, wyim, a16re, a16im, d_b)


def _s5_out_kernel(z_ref, wv_ref, wg_ref, h_ref, o_ref, zt_s):
    @pl.when(pl.program_id(1) == 0)
    def _():
        zt_s[...] = z_ref[...].T

    z = zt_s[...]
    v = jnp.dot(z, wv_ref[...], preferred_element_type=F32)
    gate = jnp.dot(z, wg_ref[...], preferred_element_type=F32)
    o_ref[...] = h_ref[...] + v * jax.nn.sigmoid(gate)


def _s5_out(zt, w_glu, h, slp):
    _, f, _ = zt.shape
    n, d = h.shape
    tn = min(256, d)
    nj = d // tn
    h3 = h.reshape(CHUNK, slp, d)
    out = pl.pallas_call(
        _s5_out_kernel,
        grid=(CHUNK, nj),
        in_specs=[pl.BlockSpec((None, f, slp), lambda t, j: (t, 0, 0)),
                  pl.BlockSpec((f, tn), lambda t, j: (0, j)),
                  pl.BlockSpec((f, tn), lambda t, j: (0, nj + j)),
                  pl.BlockSpec((None, slp, tn), lambda t, j: (t, 0, j))],
        out_specs=pl.BlockSpec((None, slp, tn), lambda t, j: (t, 0, j)),
        out_shape=jax.ShapeDtypeStruct(h3.shape, F32),
        scratch_shapes=[pltpu.VMEM((slp, f), BF16)],
        compiler_params=_cparams("parallel", "arbitrary"),
        name="s5_out",
    )(zt, w_glu, w_glu, h3)
    return out.reshape(n, d)


def _pool_body(win, first, u_ref, halo_ref, w_ref, sc_ref, h_ref, o_ref, nb):
    rows = u_ref.shape[1]
    row = lax.broadcasted_iota(I32, (rows, 1), 0)

    def cur(t):
        return u_ref[t]

    def prev(t):
        halo = jnp.where(first, 0.0, halo_ref[t])
        return jnp.concatenate([halo, u_ref[t, 0:rows - nb, :]], axis=0)

    def at(tau):
        return cur(tau) if tau >= 0 else prev(tau + CHUNK)

    run = at(0)
    for j in range(1, win):
        run = run + at(-j)
    wmat = w_ref[...]
    scale = sc_ref[...]
    for t in range(CHUNK):
        if t > 0:
            run = run + at(t) - at(t - win)
        if t + 1 >= win:
            inv = 1.0 / win
        else:
            inv = jnp.where(jnp.logical_and(first, row < nb), 1.0 / (t + 1), 1.0 / win)
        mixed = run * inv - cur(t)
        y = jnp.dot(mixed.astype(BF16), wmat, preferred_element_type=F32)
        o_ref[t] = h_ref[t] + y * scale


def _pool_kernel(u_ref, halo_ref, w_ref, sc_ref, h_ref, o_ref, *, nb, n_real):
    g = pl.program_id(0)
    first = pl.program_id(1) == n_real
    for gi, win in enumerate(POOL_WINDOWS):
        @pl.when(g == gi)
        def _(win=win):
            _pool_body(win, first, u_ref, halo_ref, w_ref, sc_ref, h_ref, o_ref, nb)


def _pool_core(u, w_grp, scale, h, nb, slp, nc):
    n, d = h.shape
    ng = len(POOL_WINDOWS)
    cw = d // ng
    rt = LANE
    per = rt // nb
    u3 = u.reshape(CHUNK, slp, d)
    h3 = h.reshape(CHUNK, slp, d)
    blk = pl.BlockSpec((CHUNK, rt, cw), lambda g, i: (0, i, g))

    def halo_map(g, i):
        return (0, jnp.where(i == 0, nc, jnp.maximum(i * per - 1, 0)), g)

    out = pl.pallas_call(
        functools.partial(_pool_kernel, nb=nb, n_real=nc // per),
        grid=(ng, slp // rt),
        in_specs=[blk,
                  pl.BlockSpec((CHUNK, nb, cw), halo_map),
                  pl.BlockSpec((None, cw, cw), lambda g, i: (g, 0, 0)),
                  pl.BlockSpec((1, cw), lambda g, i: (0, g)),
                  blk],
        out_specs=blk,
        out_shape=jax.ShapeDtypeStruct(h3.shape, F32),
        compiler_params=_cparams("parallel", "parallel"),
        name="pool_core",
    )(u3, u3, w_grp, scale.reshape(1, d), h3)
    return out.reshape(n, d)


def _router_kernel(h_ref, g_ref, whi_ref, wlo_ref, b_ref, eid_ref, wts_ref, rank_ref, cnt_ref, carry_s):
    i = pl.program_id(0)
    tm = h_ref.shape[0]

    @pl.when(i == 0)
    def _():
        carry_s[...] = jnp.zeros_like(carry_s)

    xn = _rms(h_ref[...], g_ref[...])
    x_hi = xn.astype(BF16)
    x_lo = (xn - x_hi.astype(F32)).astype(BF16)
    w_hi = whi_ref[...]
    lt = (jnp.dot(x_hi, w_hi, preferred_element_type=F32)
          + jnp.dot(x_lo, w_hi, preferred_element_type=F32)
          + jnp.dot(x_hi, wlo_ref[...], preferred_element_type=F32))
    logits = lt.T + b_ref[:, 0:1]

    c = [logits[r:r + 1] for r in range(N_EXPERT_GROUPS)]
    cmax = jnp.maximum(jnp.maximum(c[0], c[1]), jnp.maximum(c[2], c[3]))
    grp = jnp.where(c[0] == cmax, 0, jnp.where(c[1] == cmax, 1, jnp.where(c[2] == cmax, 2, 3)))
    csum = (jnp.exp(c[0] - cmax) + jnp.exp(c[1] - cmax)) + (jnp.exp(c[2] - cmax) + jnp.exp(c[3] - cmax))
    p_grp = 1.0 / csum

    epg = EXPERTS_PER_GROUP
    sel = jnp.zeros((epg, tm), F32)
    for gi in range(N_EXPERT_GROUPS):
        blk = logits[FINE_ROW0 + gi * epg:FINE_ROW0 + (gi + 1) * epg]
        sel = jnp.where(grp == gi, blk, sel)
    ridx = lax.broadcasted_iota(I32, (epg, tm), 0)
    m1 = jnp.max(sel, axis=0, keepdims=True)
    i1 = jnp.min(jnp.where(sel == m1, ridx, epg), axis=0, keepdims=True)
    sel2 = jnp.where(ridx == i1, -jnp.inf, sel)
    m2 = jnp.max(sel2, axis=0, keepdims=True)
    i2 = jnp.min(jnp.where(sel2 == m2, ridx, epg), axis=0, keepdims=True)
    ssum = jnp.sum(jnp.exp(sel - m1), axis=0, keepdims=True)
    p1 = 1.0 / ssum
    p2 = jnp.exp(m2 - m1) / ssum
    den = p1 + p2
    e1 = grp * epg + i1
    e2 = grp * epg + i2
    eid_ref[0:1, :] = e1
    eid_ref[1:2, :] = e2
    wts_ref[0:1, :] = p1 / den * p_grp
    wts_ref[1:2, :] = p2 / den * p_grp

    eidx = lax.broadcasted_iota(I32, (N_EXPERTS, tm), 0)
    hit1 = eidx == e1
    hit2 = eidx == e2
    onehot = jnp.where(hit1, 1.0, jnp.where(hit2, 1.0, 0.0))
    tri = jnp.where(lax.broadcasted_iota(I32, (tm, tm), 0) <= lax.broadcasted_iota(I32, (tm, tm), 1),
                    1.0, 0.0).astype(BF16)
    pref = jnp.dot(onehot.astype(BF16), tri, preferred_element_type=F32)
    tot = pref + carry_s[:, 0:1]
    rank_ref[0:1, :] = (jnp.sum(jnp.where(hit1, tot, 0.0), axis=0, keepdims=True) - 1.0).astype(I32)
    rank_ref[1:2, :] = (jnp.sum(jnp.where(hit2, tot, 0.0), axis=0, keepdims=True) - 1.0).astype(I32)
    carry_s[...] = carry_s[...] + pref[:, tm - 1:tm]
    cnt_ref[...] = carry_s[...]


def _router(h, g, w_hi, w_lo, bias):
    n, d = h.shape
    tm = TOK_TILE
    two = pl.BlockSpec((2, tm), lambda i: (0, i))
    return pl.pallas_call(
        _router_kernel,
        grid=(n // tm,),
        in_specs=[pl.BlockSpec((tm, d), lambda i: (i, 0)),
                  pl.BlockSpec((1, d), lambda i: (0, 0)),
                  pl.BlockSpec((d, ROUTER_ROWS), lambda i: (0, 0)),
                  pl.BlockSpec((d, ROUTER_ROWS), lambda i: (0, 0)),
                  pl.BlockSpec((ROUTER_ROWS, LANE), lambda i: (0, 0))],
        out_specs=[two, two, two, pl.BlockSpec((N_EXPERTS, LANE), lambda i: (0, 0))],
        out_shape=[jax.ShapeDtypeStruct((2, n), I32), jax.ShapeDtypeStruct((2, n), F32),
                   jax.ShapeDtypeStruct((2, n), I32), jax.ShapeDtypeStruct((N_EXPERTS, LANE), F32)],
        scratch_shapes=[pltpu.VMEM((N_EXPERTS, LANE), F32)],
        compiler_params=_cparams("arbitrary"),
        name="moe_router",
    )(h, g.reshape(1, d), w_hi, w_lo, bias)


def _router_params(w_coarse, b_coarse, w_fine, b_fine):
    d = w_coarse.shape[0]
    pad0 = jnp.zeros((d, FINE_ROW0 - N_EXPERT_GROUPS), F32)
    pad1 = jnp.zeros((d, ROUTER_ROWS - FINE_ROW0 - N_EXPERTS), F32)
    wr = jnp.concatenate([w_coarse.astype(F32), pad0, w_fine.astype(F32), pad1], axis=1)
    w_hi = wr.astype(BF16)
    w_lo = (wr - w_hi.astype(F32)).astype(BF16)
    bias = jnp.concatenate([b_coarse.astype(F32), pad0[0], b_fine.astype(F32), pad1[0]])
    return w_hi, w_lo, jnp.broadcast_to(bias[:, None], (ROUTER_ROWS, LANE))


def _route_tables(counts, n_tiles):
    cnt = counts[:, 0].astype(I32)
    tiles_e = (cnt + EXP_TILE - 1) // EXP_TILE
    tile_end = jnp.cumsum(tiles_e)
    tile_start = tile_end - tiles_e
    n_used = tile_end[-1]
    tq = jnp.minimum(jnp.arange(n_tiles, dtype=I32), n_used - 1)
    te = jnp.sum((tile_end[None, :] <= tq[:, None]).astype(I32), axis=1)
    onehot = (te[:, None] == jnp.arange(N_EXPERTS, dtype=I32)[None, :]).astype(I32)
    cnt_t = jnp.sum(onehot * cnt[None, :], axis=1)
    start_t = jnp.sum(onehot * tile_start[None, :], axis=1)
    tid = jnp.arange(n_tiles, dtype=I32)
    valid = jnp.where(tid < n_used, jnp.clip(cnt_t - (tid - start_t) * EXP_TILE, 0, EXP_TILE), 0)
    seg_start = jnp.broadcast_to((tile_start * EXP_TILE)[:, None], (N_EXPERTS, LANE))
    pad_start = tile_start * EXP_TILE + cnt
    return (seg_start.astype(I32), pad_start.astype(I32), te.astype(I32), valid.astype(I32),
            jnp.reshape(n_used, (1,)).astype(I32))


def _pos_kernel(eid_ref, rank_ref, st_ref, pos_ref):
    tm = eid_ref.shape[1]
    eidx = lax.broadcasted_iota(I32, (N_EXPERTS, tm), 0)
    st = st_ref[:, 0:1]
    for k in range(2):
        base = jnp.sum(jnp.where(eidx == eid_ref[k:k + 1, :], st, 0), axis=0, keepdims=True)
        pos_ref[k:k + 1, :] = base + rank_ref[k:k + 1, :]


def _positions(eid, rank, seg_start):
    n = eid.shape[1]
    tm = TOK_TILE
    two = pl.BlockSpec((2, tm), lambda i: (0, i))
    return pl.pallas_call(
        _pos_kernel,
        grid=(n // tm,),
        in_specs=[two, two, pl.BlockSpec((N_EXPERTS, LANE), lambda i: (0, 0))],
        out_specs=pl.BlockSpec((None, 2, tm), lambda i: (i, 0, 0)),
        out_shape=jax.ShapeDtypeStruct((n // tm, 2, tm), I32),
        compiler_params=_cparams("parallel"),
        name="moe_positions",
    )(eid, rank, seg_start)


SUB = 8


def _dispatch_kernel(pad_ref, nu_ref, pos_ref, h_ref, g_ref, xs_ref, buf, zbuf, sem, zsem):
    i = pl.program_id(0)
    n = pl.num_programs(0)
    tm = h_ref.shape[0]
    half = h_ref.shape[1] // 2
    slot = i % 2

    def wait_slot(s):
        for _ in range(2):
            pltpu.make_async_copy(zbuf, xs_ref.at[pl.ds(0, tm)], sem.at[s]).wait()

    @pl.when(i == 0)
    def _():
        zbuf[...] = jnp.zeros_like(zbuf)
        zt = zbuf.shape[0]
        sizes = [1 << k for k in range(zt.bit_length() - 1)]

        def pad_pieces(e, act):
            off = pad_ref[e]
            length = (zt - (off & (zt - 1))) & (zt - 1)
            for sz in sizes:
                take = (length & sz) != 0

                @pl.when(take)
                def _(off=off, sz=sz):
                    if sz < SUB:
                        for r in range(sz):
                            act(pltpu.make_async_copy(zbuf.at[pl.ds(0, 1)], xs_ref.at[pl.ds(off + r, 1)], zsem))
                    else:
                        act(pltpu.make_async_copy(zbuf.at[pl.ds(0, sz)],
                                                  xs_ref.at[pl.ds(pl.multiple_of(off, SUB), sz)], zsem))

                off = off + jnp.where(take, sz, 0)

        def tail_piece(j, act):
            act(pltpu.make_async_copy(zbuf, xs_ref.at[pl.ds(pl.multiple_of(j * zt, zt), zt)], zsem))

        n_all = xs_ref.shape[0] // zt
        for act in (lambda c: c.start(), lambda c: c.wait()):
            lax.fori_loop(0, N_EXPERTS, lambda e, carry, act=act: (pad_pieces(e, act), carry)[1], 0)
            lax.fori_loop(nu_ref[0], n_all, lambda j, carry, act=act: (tail_piece(j, act), carry)[1], 0)

    @pl.when(i >= 2)
    def _():
        wait_slot(slot)

    xn = _rms(h_ref[...], g_ref[...])
    buf[slot] = _pack_pairs(xn[:, :half], xn[:, half:]).reshape(tm // SUB, SUB, half)

    def issue(q, carry):
        for s in range(SUB):
            for k in range(2):
                pltpu.make_async_copy(buf.at[slot, q, pl.ds(s, 1)],
                                      xs_ref.at[pl.ds(pos_ref[q, 2 * s + k], 1)],
                                      sem.at[slot]).start(priority=k)
        return carry

    lax.fori_loop(0, tm // SUB, issue, 0)

    @pl.when(i == n - 1)
    def _():
        wait_slot(slot)

        @pl.when(n >= 2)
        def _():
            wait_slot(1 - slot)


def _dispatch(h, g, pos3, pad_start, n_used, n_rows, word_dtype):
    n, d = h.shape
    tm = TOK_TILE
    assert tm == EXP_TILE
    half = d // 2
    grid_spec = pltpu.PrefetchScalarGridSpec(
        num_scalar_prefetch=2,
        grid=(n // tm,),
        in_specs=[pl.BlockSpec((None, tm // SUB, 2 * SUB), lambda i, pad, nu: (i, 0, 0), memory_space=pltpu.SMEM),
                  pl.BlockSpec((tm, d), lambda i, pad, nu: (i, 0)),
                  pl.BlockSpec((1, d), lambda i, pad, nu: (0, 0))],
        out_specs=pl.BlockSpec(memory_space=pl.ANY),
        scratch_shapes=[pltpu.VMEM((2, tm // SUB, SUB, half), word_dtype),
                        pltpu.VMEM((EXP_TILE, half), word_dtype),
                        pltpu.SemaphoreType.DMA((2,)), pltpu.SemaphoreType.DMA(())],
    )
    return pl.pallas_call(
        _dispatch_kernel,
        grid_spec=grid_spec,
        out_shape=jax.ShapeDtypeStruct((n_rows, half), word_dtype),
        compiler_params=_cparams("arbitrary"),
        name="moe_dispatch",
    )(pad_start, n_used, pos3, h, g.reshape(1, d))


def _expert_kernel(te_ref, va_ref, nu_ref, xs_ref, wg_ref, wu_ref, wd_ref, ys_ref, wgb, wub, wdb):
    del nu_ref
    i = pl.program_id(0)
    changed = jnp.logical_or(i == 0, te_ref[i] != te_ref[jnp.maximum(i - 1, 0)])

    @pl.when(changed)
    def _():
        wgb[...] = wg_ref[...].astype(BF16)
        wub[...] = wu_ref[...].astype(BF16)
        wdb[...] = wd_ref[...].astype(BF16)

    valid = va_ref[i]

    @pl.when(valid > 0)
    def _():
        lo, hi = _unpack_pairs(xs_ref[...])
        lo = lo.astype(BF16)
        hi = hi.astype(BF16)
        half = lo.shape[1]
        gate = (jnp.dot(lo, wgb[0:half, :], preferred_element_type=F32)
                + jnp.dot(hi, wgb[half:2 * half, :], preferred_element_type=F32))
        up = (jnp.dot(lo, wub[0:half, :], preferred_element_type=F32)
              + jnp.dot(hi, wub[half:2 * half, :], preferred_element_type=F32))
        hid = (gate * jax.nn.sigmoid(gate)) * up
        y = jnp.dot(hid.astype(BF16), wdb[...], preferred_element_type=F32)
        ys_ref[...] = _pack_pairs(y[:, :half], y[:, half:])

    @pl.when(valid <= 0)
    def _():
        ys_ref[...] = jnp.zeros_like(ys_ref)


def _experts(xs, te, valid, n_used, w_gate, w_up, w_down, layer):
    n_tiles = te.shape[0]
    _, half = xs.shape
    _, _, d, de = w_gate.shape
    te_tile = EXP_TILE
    grid_spec = pltpu.PrefetchScalarGridSpec(
        num_scalar_prefetch=3,
        grid=(n_tiles,),
        in_specs=[pl.BlockSpec((te_tile, half), lambda i, te, va, nu: (jnp.minimum(i, nu[0] - 1), 0)),
                  pl.BlockSpec((None, None, d, de), lambda i, te, va, nu: (layer, te[i], 0, 0)),
                  pl.BlockSpec((None, None, d, de), lambda i, te, va, nu: (layer, te[i], 0, 0)),
                  pl.BlockSpec((None, None, de, d), lambda i, te, va, nu: (layer, te[i], 0, 0))],
        out_specs=pl.BlockSpec((te_tile, half), lambda i, te, va, nu: (i, 0)),
        scratch_shapes=[pltpu.VMEM((d, de), BF16), pltpu.VMEM((d, de), BF16), pltpu.VMEM((de, d), BF16)],
    )
    return pl.pallas_call(
        _expert_kernel,
        grid_spec=grid_spec,
        out_shape=jax.ShapeDtypeStruct((n_tiles * te_tile, half), xs.dtype),
        compiler_params=_cparams("arbitrary"),
        name="moe_experts",
    )(te, valid, n_used, xs, w_gate, w_up, w_down)


def _combine_kernel(pos_ref, posn_ref, w_ref, h_ref, *rest, with_norm):
    if with_norm:
        g_ref, ys_ref, o_ref, hn_ref, buf, sem = rest
    else:
        ys_ref, o_ref, buf, sem = rest
    i = pl.program_id(0)
    n = pl.num_programs(0)
    tm = h_ref.shape[0]
    half = h_ref.shape[1] // 2
    slot = i % 2

    def issue(p_ref, s):
        def body(q, carry):
            for sub in range(SUB):
                for k in range(2):
                    pltpu.make_async_copy(ys_ref.at[pl.ds(p_ref[q, 2 * sub + k], 1)],
                                          buf.at[s, k, q, pl.ds(sub, 1)], sem.at[s]).start(priority=k)
            return carry
        lax.fori_loop(0, tm // SUB, body, 0)

    @pl.when(i == 0)
    def _():
        issue(pos_ref, 0)

    @pl.when(i + 1 < n)
    def _():
        issue(posn_ref, 1 - slot)

    for k in range(2):
        for _ in range(tm // SUB):
            pltpu.make_async_copy(ys_ref.at[pl.ds(0, SUB)], buf.at[slot, k, 0], sem.at[slot]).wait()

    lo0, hi0 = _unpack_pairs(buf[slot, 0].reshape(tm, half))
    lo1, hi1 = _unpack_pairs(buf[slot, 1].reshape(tm, half))
    w0 = jnp.broadcast_to(w_ref[0:1, :], (LANE, tm)).T[:, 0:1]
    w1 = jnp.broadcast_to(w_ref[1:2, :], (LANE, tm)).T[:, 0:1]
    h = h_ref[...]
    hnew = jnp.concatenate([h[:, :half] + (w0 * lo0 + w1 * lo1),
                            h[:, half:] + (w0 * hi0 + w1 * hi1)], axis=1)
    o_ref[...] = hnew
    if with_norm:
        hn_ref[...] = _rms(hnew, g_ref[...]).astype(hn_ref.dtype)


def _combine(h, ys, pos3, wts, g_next):
    n, d = h.shape
    tm = TOK_TILE
    nt = n // tm
    half = ys.shape[1]
    with_norm = g_next is not None
    smem = functools.partial(pl.BlockSpec, (None, tm // SUB, 2 * SUB), memory_space=pltpu.SMEM)
    row = pl.BlockSpec((tm, d), lambda i: (i, 0))
    in_specs = [smem(lambda i: (i, 0, 0)),
                smem(lambda i: (jnp.minimum(i + 1, nt - 1), 0, 0)),
                pl.BlockSpec((2, tm), lambda i: (0, i)),
                row]
    args = [pos3, pos3, wts, h]
    out_specs = [row]
    out_shape = [jax.ShapeDtypeStruct((n, d), F32)]
    if with_norm:
        in_specs.append(pl.BlockSpec((1, d), lambda i: (0, 0)))
        args.append(g_next.reshape(1, d))
        out_specs.append(row)
        out_shape.append(jax.ShapeDtypeStruct((n, d), BF16))
    in_specs.append(pl.BlockSpec(memory_space=pl.ANY))
    args.append(ys)
    out = pl.pallas_call(
        functools.partial(_combine_kernel, with_norm=with_norm),
        grid=(nt,),
        in_specs=in_specs,
        out_specs=out_specs,
        out_shape=out_shape,
        scratch_shapes=[pltpu.VMEM((2, 2, tm // SUB, SUB, half), ys.dtype), pltpu.SemaphoreType.DMA((2,))],
        compiler_params=_cparams("arbitrary"),
        name="moe_combine",
    )(*args)
    return (out[0], out[1]) if with_norm else (out[0], None)


def _hier_moe(h, g_ffn, g_next, layer, w_coarse, b_coarse, w_fine, b_fine, w_gate, w_up, w_down):
    n = h.shape[0]
    n_tiles = (2 * n + N_EXPERTS * (EXP_TILE - 1) + EXP_TILE - 1) // EXP_TILE
    w_hi, w_lo, bias = _router_params(w_coarse, b_coarse, w_fine, b_fine)
    eid, wts, rank, counts = _router(h, g_ffn, w_hi, w_lo, bias)
    seg_start, pad_start, te, valid, n_used = _route_tables(counts, n_tiles)
    pos3 = _positions(eid, rank, seg_start)
    nt = pos3.shape[0]
    pos3 = pos3.reshape(nt, 2, TOK_TILE // SUB, SUB).transpose(0, 2, 3, 1).reshape(nt, TOK_TILE // SUB, 2 * SUB)
    word_dtype = jax.eval_shape(lambda a: _pack_pairs(a, a), jax.ShapeDtypeStruct((8, LANE), F32)).dtype
    xs = _dispatch(h, g_ffn, pos3, pad_start, n_used, n_tiles * EXP_TILE, word_dtype)
    ys = _experts(xs, te, valid, n_used, w_gate, w_up, w_down, layer)
    return _combine(h, ys, pos3, wts, g_next)


def kernel(x, meta_tokens, norm_mix, norm_ffn, norm_final, s5_w_in, s5_a_re, s5_a_im, s5_log_dt,
           s5_b_re, s5_b_im, s5_c_re, s5_c_im, s5_d, s5_w_glu, pool_w_in, pool_w_grp, pool_scale,
           moe_w_coarse, moe_b_coarse, moe_w_fine, moe_b_fine, moe_w_gate, moe_w_up, moe_w_down):
    b, seq, d = x.shape
    depth = norm_mix.shape[0]
    assert meta_tokens.shape[0] == N_META == CHUNK and seq % CHUNK == 0
    assert LANE % b == 0 and b % 16 == 0
    per = LANE // b
    nc = seq // CHUNK
    assert nc % per == 0
    ncp = nc + per
    slp = ncp * b

    h, hn = _ingest(x, meta_tokens, norm_mix[0], ncp)
    for i in range(depth):
        j = i // 2
        if i % 2 == 0:
            g = s5_a_re.shape[1]
            ops = _s5_prep(s5_a_re[j], s5_a_im[j], s5_log_dt[j], s5_b_re[j], s5_b_im[j],
                           s5_c_re[j], s5_c_im[j])
            d_b = jnp.broadcast_to(jnp.tile(s5_d[j].reshape(g, 1, S5_GROUP), (1, CHUNK, 1))
                                   .reshape(g, CHUNK_W, 1), (g, CHUNK_W, LANE))
            ut = _s5_in(hn, s5_w_in[j].astype(BF16), slp)
            zt = _s5_core(ut, ops, d_b, b, nc)
            h = _s5_out(zt, s5_w_glu[j].astype(BF16), h, slp)
        else:
            u = _matmul(hn, pool_w_in[j].astype(BF16), F32)
            h = _pool_core(u, pool_w_grp[j].astype(BF16), pool_scale[j], h, b, slp, nc)
        g_next = None if i == depth - 1 else norm_mix[i + 1]
        h, hn = _hier_moe(h, norm_ffn[i], g_next, i,
                          moe_w_coarse[i], moe_b_coarse[i], moe_w_fine[i], moe_b_fine[i],
                          moe_w_gate, moe_w_up, moe_w_down)
    return _egress(h, norm_final, b, seq, ncp)
```

```python
import functools

import jax
import jax.numpy as jnp
from jax import lax
from jax.experimental import pallas as pl
from jax.experimental.pallas import tpu as pltpu

F32 = jnp.float32
BF16 = jnp.bfloat16
I32 = jnp.int32

N_META = 16
CHUNK = 16
RMS_EPS = 1e-6
S5_GROUP = 16
S5_STATE = 64
CHUNK_W = CHUNK * S5_GROUP
POOL_WINDOWS = (2, 4, 8, 16)
N_EXPERT_GROUPS = 4
EXPERTS_PER_GROUP = 8
N_EXPERTS = N_EXPERT_GROUPS * EXPERTS_PER_GROUP
ROUTER_ROWS = 128
FINE_ROW0 = 8

LANE = 128
TOK_TILE = 256
EXP_TILE = 256
VMEM_LIMIT = 56 * 1024 * 1024

_NT = (((1,), (1,)), ((), ()))
_TN = (((0,), (0,)), ((), ()))


def _cparams(*sem):
    return pltpu.CompilerParams(dimension_semantics=sem, vmem_limit_bytes=VMEM_LIMIT)


def _rms(x, g):
    return x * lax.rsqrt(jnp.mean(x * x, axis=-1, keepdims=True) + RMS_EPS) * g


def _pack_pairs(lo, hi):
    return pltpu.pack_elementwise([lo, hi], packed_dtype=BF16)


def _unpack_pairs(w):
    lo = pltpu.unpack_elementwise(w, index=0, packed_dtype=BF16, unpacked_dtype=F32)
    hi = pltpu.unpack_elementwise(w, index=1, packed_dtype=BF16, unpacked_dtype=F32)
    return lo, hi


def _ingest_kernel(x_ref, meta_ref, g_ref, h_ref, hn_ref, *, nc):
    c = pl.program_id(0)
    g = g_ref[...]

    def emit(t, h):
        h_ref[t] = h
        hn_ref[t] = _rms(h, g).astype(hn_ref.dtype)

    @pl.when(c < nc)
    def _():
        nb, d = h_ref.shape[1], h_ref.shape[2]
        rows = nb * CHUNK
        x = x_ref[...].reshape(rows, d)
        r = lax.broadcasted_iota(I32, (rows, rows), 0)
        q = lax.broadcasted_iota(I32, (rows, rows), 1)
        lb, lt = nb.bit_length() - 1, CHUNK.bit_length() - 1
        hit = jnp.logical_and((r >> lb) == (q & (CHUNK - 1)), (r & (nb - 1)) == (q >> lt))
        perm = jnp.where(hit, 1.0, 0.0).astype(BF16)
        x1 = x.astype(BF16)
        r1 = x - x1.astype(F32)
        x2 = r1.astype(BF16)
        x3 = (r1 - x2.astype(F32)).astype(BF16)
        y = (jnp.dot(perm, x1, preferred_element_type=F32) + jnp.dot(perm, x2, preferred_element_type=F32)
             + jnp.dot(perm, x3, preferred_element_type=F32))
        for t in range(CHUNK):
            emit(t, y[t * nb:(t + 1) * nb, :])

    @pl.when(c == nc)
    def _():
        nb, d = h_ref.shape[1], h_ref.shape[2]
        for t in range(CHUNK):
            emit(t, jnp.broadcast_to(meta_ref[t:t + 1, :], (nb, d)))

    @pl.when(c > nc)
    def _():
        h_ref[...] = jnp.zeros_like(h_ref)
        hn_ref[...] = jnp.zeros_like(hn_ref)


def _ingest(x, meta, g, ncp):
    b, seq, d = x.shape
    nc = seq // CHUNK
    out = pl.BlockSpec((CHUNK, b, d), lambda c: (0, c, 0))
    h3, hn3 = pl.pallas_call(
        functools.partial(_ingest_kernel, nc=nc),
        grid=(ncp,),
        in_specs=[pl.BlockSpec((b, None, CHUNK, d), lambda c: (0, jnp.minimum(c, nc - 1), 0, 0)),
                  pl.BlockSpec((N_META, d), lambda c: (0, 0)),
                  pl.BlockSpec((1, d), lambda c: (0, 0))],
        out_specs=[out, out],
        out_shape=[jax.ShapeDtypeStruct((CHUNK, ncp * b, d), F32),
                   jax.ShapeDtypeStruct((CHUNK, ncp * b, d), BF16)],
        compiler_params=_cparams("parallel"),
        name="ingest",
    )(x.reshape(b, nc, CHUNK, d), meta.astype(F32), g.reshape(1, d))
    n = CHUNK * ncp * b
    return h3.reshape(n, d), hn3.reshape(n, d)


def _egress_kernel(h_ref, g_ref, o_ref):
    g = g_ref[...]
    for t in range(CHUNK):
        o_ref[:, t, :] = _rms(h_ref[t], g)


def _egress(h, g, b, seq, ncp):
    n, d = h.shape
    nc = seq // CHUNK
    out = pl.pallas_call(
        _egress_kernel,
        grid=(nc,),
        in_specs=[pl.BlockSpec((CHUNK, b, d), lambda c: (0, c, 0)),
                  pl.BlockSpec((1, d), lambda c: (0, 0))],
        out_specs=pl.BlockSpec((b, None, CHUNK, d), lambda c: (0, c, 0, 0)),
        out_shape=jax.ShapeDtypeStruct((b, nc, CHUNK, d), F32),
        compiler_params=_cparams("parallel"),
        name="egress",
    )(h.reshape(CHUNK, ncp * b, d), g.reshape(1, d))
    return out.reshape(b, seq, d)


def _mm_kernel(x_ref, w_ref, o_ref):
    o_ref[...] = jnp.dot(x_ref[...], w_ref[...], preferred_element_type=F32).astype(o_ref.dtype)


def _matmul(x, w, out_dtype, tm=512):
    n, k = x.shape
    f = w.shape[1]
    return pl.pallas_call(
        _mm_kernel,
        grid=(n // tm,),
        in_specs=[pl.BlockSpec((tm, k), lambda i: (i, 0)),
                  pl.BlockSpec((k, f), lambda i: (0, 0))],
        out_specs=pl.BlockSpec((tm, f), lambda i: (i, 0)),
        out_shape=jax.ShapeDtypeStruct((n, f), out_dtype),
        compiler_params=_cparams("parallel"),
        name="pool_in",
    )(x, w)


def _mm_t_kernel(x_ref, w_ref, o_ref):
    o_ref[...] = jnp.dot(x_ref[...], w_ref[...], preferred_element_type=F32).astype(o_ref.dtype).T


def _s5_in(hn, w, slp):
    n, k = hn.shape
    f = w.shape[1]
    tf = min(512, f)
    return pl.pallas_call(
        _mm_t_kernel,
        grid=(CHUNK, f // tf),
        in_specs=[pl.BlockSpec((None, slp, k), lambda t, j: (t, 0, 0)),
                  pl.BlockSpec((k, tf), lambda t, j: (0, j))],
        out_specs=pl.BlockSpec((None, tf, slp), lambda t, j: (t, j, 0)),
        out_shape=jax.ShapeDtypeStruct((CHUNK, f, slp), BF16),
        compiler_params=_cparams("parallel", "arbitrary"),
        name="s5_in",
    )(hn.reshape(CHUNK, slp, k), w)


def _s5_prep_kernel(lrc_ref, lic_ref, ldt_ref, brt_ref, bit_ref, cre_ref, cim_ref,
                    lrr_ref, lir_ref, crt_ref, cit_ref,
                    m_ref, wz_ref, wyre_ref, wyim_ref, a16re_ref, a16im_ref):
    g = pl.program_id(0)
    p = S5_STATE
    dt = jnp.exp(ldt_ref[0:1, :])
    lr = lrc_ref[...]
    li = lic_ref[...]
    lam_r = lr * dt
    lam_i = li * dt
    mag = jnp.exp(lam_r)
    abr = mag * jnp.cos(lam_i)
    abi = mag * jnp.sin(lam_i)
    den = lr * lr + li * li
    nr = abr - 1.0
    cfr = (nr * lr + abi * li) / den
    cfi = (abi * lr - nr * li) / den

    def two(x):
        return jnp.concatenate([x, x], axis=1)

    def powers(ar, ai, n):
        out = [(ar, ai)]
        for _ in range(n - 1):
            qr, qi = out[-1]
            out.append((qr * ar - qi * ai, qr * ai + qi * ar))
        return out

    cfr2, cfi2 = two(cfr), two(cfi)
    brt = brt_ref[...]
    bit = bit_ref[...]
    bbr = cfr2 * brt - cfi2 * bit
    bbi = cfr2 * bit + cfi2 * brt
    lane = lax.broadcasted_iota(I32, (1, CHUNK_W), 1)
    t_in = lane >> 4
    colp = powers(abr, abi, CHUNK - 1)
    pr = jnp.where(t_in == CHUNK - 1, 1.0, 0.0) + jnp.zeros_like(bbr)
    pi = jnp.zeros_like(bbr)
    for t in range(CHUNK - 1):
        qr, qi = colp[CHUNK - 2 - t]
        pr = jnp.where(t_in == t, two(qr), pr)
        pi = jnp.where(t_in == t, two(qi), pi)
    wzr = pr * bbr - pi * bbi
    wzi = pr * bbi + pi * bbr
    wz_ref[0:p, :] = wzr.astype(BF16)
    wz_ref[p:2 * p, :] = wzi.astype(BF16)

    krev = (jnp.dot(cre_ref[...], wzr, precision=lax.Precision.HIGHEST, preferred_element_type=F32)
            - jnp.dot(cim_ref[...], wzi, precision=lax.Precision.HIGHEST, preferred_element_type=F32))
    for t in range(CHUNK):
        s = (CHUNK - 1 - t) * S5_GROUP
        blk = krev if s == 0 else pltpu.roll(krev, CHUNK_W - s, 1)
        blk = jnp.where(lane < CHUNK_W - s, blk, 0.0)
        m_ref[t * S5_GROUP:(t + 1) * S5_GROUP, :] = blk.astype(BF16)

    lamr_row = lrr_ref[0:1, :] * dt
    lami_row = lir_ref[0:1, :] * dt
    mag_row = jnp.exp(lamr_row)
    rowp = powers(mag_row * jnp.cos(lami_row), mag_row * jnp.sin(lami_row), CHUNK)
    pwr = jnp.concatenate([jnp.broadcast_to(q[0], (S5_GROUP, LANE)) for q in rowp], axis=0)
    pwi = jnp.concatenate([jnp.broadcast_to(q[1], (S5_GROUP, LANE)) for q in rowp], axis=0)
    cr2 = crt_ref[...]
    ci2 = cit_ref[...]
    lane128 = lax.broadcasted_iota(I32, (1, LANE), 1)
    sel = (lane128 >> 6) == (g % 2)
    wyre_ref[...] = jnp.where(sel, cr2 * pwr - ci2 * pwi, 0.0).astype(BF16)
    wyim_ref[...] = jnp.where(sel, -(cr2 * pwi) - ci2 * pwr, 0.0).astype(BF16)

    a16re_ref[...] = jnp.broadcast_to(rowp[CHUNK - 1][0], (8, LANE))
    a16im_ref[...] = jnp.broadcast_to(rowp[CHUNK - 1][1], (8, LANE))


def _s5_prep(a_re, a_im, log_dt, b_re, b_im, c_re, c_im):
    g, p = a_re.shape
    lrc = jnp.broadcast_to(a_re[:, :, None], (g, p, LANE))
    lic = jnp.broadcast_to(a_im[:, :, None], (g, p, LANE))
    ldt = jnp.broadcast_to(log_dt[:, None, None], (g, 8, LANE))
    brt = jnp.tile(b_re, (1, 1, CHUNK))
    bit = jnp.tile(b_im, (1, 1, CHUNK))
    lrr = jnp.broadcast_to(jnp.concatenate([a_re, a_re], axis=1)[:, None, :], (g, 8, LANE))
    lir = jnp.broadcast_to(jnp.concatenate([a_im, a_im], axis=1)[:, None, :], (g, 8, LANE))
    crt = jnp.tile(jnp.concatenate([c_re, c_re], axis=2), (1, CHUNK, 1))
    cit = jnp.tile(jnp.concatenate([c_im, c_im], axis=2), (1, CHUNK, 1))

    def spec(r, c):
        return pl.BlockSpec((None, r, c), lambda i: (i, 0, 0))

    return pl.pallas_call(
        _s5_prep_kernel,
        grid=(g,),
        in_specs=[spec(p, LANE), spec(p, LANE), spec(8, LANE), spec(p, CHUNK_W), spec(p, CHUNK_W),
                  spec(S5_GROUP, p), spec(S5_GROUP, p), spec(8, LANE), spec(8, LANE),
                  spec(CHUNK_W, LANE), spec(CHUNK_W, LANE)],
        out_specs=[spec(CHUNK_W, CHUNK_W), spec(2 * p, CHUNK_W), spec(CHUNK_W, LANE),
                   spec(CHUNK_W, LANE), spec(8, LANE), spec(8, LANE)],
        out_shape=[jax.ShapeDtypeStruct((g, CHUNK_W, CHUNK_W), BF16),
                   jax.ShapeDtypeStruct((g, 2 * p, CHUNK_W), BF16),
                   jax.ShapeDtypeStruct((g, CHUNK_W, LANE), BF16),
                   jax.ShapeDtypeStruct((g, CHUNK_W, LANE), BF16),
                   jax.ShapeDtypeStruct((g, 8, LANE), F32),
                   jax.ShapeDtypeStruct((g, 8, LANE), F32)],
        compiler_params=_cparams("parallel"),
        name="s5_prep",
    )(lrc, lic, ldt, brt, bit, c_re, c_im, lrr, lir, crt, cit)


def _gelu_tanh(x):
    return x * (0.5 * (1.0 + jnp.tanh(0.7978845608028654 * (x + 0.044715 * (x * x * x)))))


def _s5_core_kernel(ut_ref, m_ref, wz_ref, wyre_ref, wyim_ref, are_ref, aim_ref, d_ref,
                    o_ref, zre_s, zim_s, spre_s, spim_s, *, nb, nc):
    p = S5_STATE
    slp = ut_ref.shape[2]
    vs, zs = [], []
    for q in range(2):
        v = ut_ref[:, q * S5_GROUP:(q + 1) * S5_GROUP, :].reshape(CHUNK_W, slp)
        vs.append(v)
        zs.append(jnp.dot(wz_ref[q], v, preferred_element_type=F32))
    zre_s[...] = jnp.concatenate([zs[0][0:p], zs[1][0:p]], axis=0).T
    zim_s[...] = jnp.concatenate([zs[0][p:2 * p], zs[1][p:2 * p]], axis=0).T

    lane = lax.broadcasted_iota(I32, (1, LANE), 1)
    are = jnp.where(lane < p, are_ref[0, 0:1, :], are_ref[1, 0:1, :])
    aim = jnp.where(lane < p, aim_ref[0, 0:1, :], aim_ref[1, 0:1, :])

    def step(c, carry):
        sr, si = carry
        r0 = pl.multiple_of(c * nb, nb)
        spre_s[pl.ds(r0, nb), :] = sr.astype(BF16)
        spim_s[pl.ds(r0, nb), :] = si.astype(BF16)
        zr = zre_s[pl.ds(r0, nb), :]
        zi = zim_s[pl.ds(r0, nb), :]
        return are * sr - aim * si + zr, are * si + aim * sr + zi

    rm = nc * nb
    spre_s[rm:slp, :] = jnp.zeros((slp - rm, LANE), BF16)
    spim_s[rm:slp, :] = jnp.zeros((slp - rm, LANE), BF16)
    lax.fori_loop(0, nc, step, (zre_s[rm:rm + nb, :], zim_s[rm:rm + nb, :]))

    spre = spre_s[...]
    spim = spim_s[...]
    for q in range(2):
        y_in = jnp.dot(m_ref[q], vs[q], preferred_element_type=F32)
        y_st = (lax.dot_general(wyre_ref[q], spre, _NT, preferred_element_type=F32)
                + lax.dot_general(wyim_ref[q], spim, _NT, preferred_element_type=F32))
        d = d_ref[q]
        for jb in range(slp // LANE):
            sl = slice(jb * LANE, (jb + 1) * LANE)
            y = y_in[:, sl] + y_st[:, sl] + d * vs[q][:, sl].astype(F32)
            z = _gelu_tanh(y).astype(BF16)
            o_ref[:, q * S5_GROUP:(q + 1) * S5_GROUP, sl] = z.reshape(CHUNK, S5_GROUP, LANE)


def _s5_core(ut, ops, d_b, nb, nc):
    m, wz, wyre, wyim, a16re, a16im = ops
    _, f, slp = ut.shape
    g = f // S5_GROUP

    def pair(r, c):
        return pl.BlockSpec((2, r, c), lambda i: (i, 0, 0))

    io_spec = pl.BlockSpec((CHUNK, 2 * S5_GROUP, slp), lambda i: (0, i, 0))
    return pl.pallas_call(
        functools.partial(_s5_core_kernel, nb=nb, nc=nc),
        grid=(g // 2,),
        in_specs=[io_spec, pair(CHUNK_W, CHUNK_W), pair(2 * S5_STATE, CHUNK_W),
                  pair(CHUNK_W, LANE), pair(CHUNK_W, LANE), pair(8, LANE), pair(8, LANE),
                  pair(CHUNK_W, LANE)],
        out_specs=io_spec,
        out_shape=jax.ShapeDtypeStruct(ut.shape, BF16),
        scratch_shapes=[pltpu.VMEM((slp, LANE), F32), pltpu.VMEM((slp, LANE), F32),
                        pltpu.VMEM((slp, LANE), BF16), pltpu.VMEM((slp, LANE), BF16)],
        compiler_params=_cparams("parallel"),
        name="s5_core",
    )(ut, m, wz, wyre, wyim, a16re, a16im, d_b)


def _s5_out_kernel(z_ref, wv_ref, wg_ref, h_ref, o_ref, zt_s):
    @pl.when(pl.program_id(1) == 0)
    def _():
        zt_s[...] = z_ref[...].T

    z = zt_s[...]
    v = jnp.dot(z, wv_ref[...], preferred_element_type=F32)
    gate = jnp.dot(z, wg_ref[...], preferred_element_type=F32)
    o_ref[...] = h_ref[...] + v * jax.nn.sigmoid(gate)


def _s5_out(zt, w_glu, h, slp):
    _, f, _ = zt.shape
    n, d = h.shape
    tn = min(256, d)
    nj = d // tn
    h3 = h.reshape(CHUNK, slp, d)
    out = pl.pallas_call(
        _s5_out_kernel,
        grid=(CHUNK, nj),
        in_specs=[pl.BlockSpec((None, f, slp), lambda t, j: (t, 0, 0)),
                  pl.BlockSpec((f, tn), lambda t, j: (0, j)),
                  pl.BlockSpec((f, tn), lambda t, j: (0, nj + j)),
                  pl.BlockSpec((None, slp, tn), lambda t, j: (t, 0, j))],
        out_specs=pl.BlockSpec((None, slp, tn), lambda t, j: (t, 0, j)),
        out_shape=jax.ShapeDtypeStruct(h3.shape, F32),
        scratch_shapes=[pltpu.VMEM((slp, f), BF16)],
        compiler_params=_cparams("parallel", "arbitrary"),
        name="s5_out",
    )(zt, w_glu, w_glu, h3)
    return out.reshape(n, d)


def _pool_body(win, first, u_ref, halo_ref, w_ref, sc_ref, h_ref, o_ref, nb):
    rows = u_ref.shape[1]
    row = lax.broadcasted_iota(I32, (rows, 1), 0)

    def cur(t):
        return u_ref[t]

    def prev(t):
        halo = jnp.where(first, 0.0, halo_ref[t])
        return jnp.concatenate([halo, u_ref[t, 0:rows - nb, :]], axis=0)

    def at(tau):
        return cur(tau) if tau >= 0 else prev(tau + CHUNK)

    run = at(0)
    for j in range(1, win):
        run = run + at(-j)
    wmat = w_ref[...]
    scale = sc_ref[...]
    for t in range(CHUNK):
        if t > 0:
            run = run + at(t) - at(t - win)
        if t + 1 >= win:
            inv = 1.0 / win
        else:
            inv = jnp.where(jnp.logical_and(first, row < nb), 1.0 / (t + 1), 1.0 / win)
        mixed = run * inv - cur(t)
        y = jnp.dot(mixed.astype(BF16), wmat, preferred_element_type=F32)
        o_ref[t] = h_ref[t] + y * scale


def _pool_kernel(u_ref, halo_ref, w_ref, sc_ref, h_ref, o_ref, *, nb, n_real):
    g = pl.program_id(0)
    first = pl.program_id(1) == n_real
    for gi, win in enumerate(POOL_WINDOWS):
        @pl.when(g == gi)
        def _(win=win):
            _pool_body(win, first, u_ref, halo_ref, w_ref, sc_ref, h_ref, o_ref, nb)


def _pool_core(u, w_grp, scale, h, nb, slp, nc):
    n, d = h.shape
    ng = len(POOL_WINDOWS)
    cw = d // ng
    rt = LANE
    per = rt // nb
    u3 = u.reshape(CHUNK, slp, d)
    h3 = h.reshape(CHUNK, slp, d)
    blk = pl.BlockSpec((CHUNK, rt, cw), lambda g, i: (0, i, g))

    def halo_map(g, i):
        return (0, jnp.where(i == 0, nc, jnp.maximum(i * per - 1, 0)), g)

    out = pl.pallas_call(
        functools.partial(_pool_kernel, nb=nb, n_real=nc // per),
        grid=(ng, slp // rt),
        in_specs=[blk,
                  pl.BlockSpec((CHUNK, nb, cw), halo_map),
                  pl.BlockSpec((None, cw, cw), lambda g, i: (g, 0, 0)),
                  pl.BlockSpec((1, cw), lambda g, i: (0, g)),
                  blk],
        out_specs=blk,
        out_shape=jax.ShapeDtypeStruct(h3.shape, F32),
        compiler_params=_cparams("parallel", "parallel"),
        name="pool_core",
    )(u3, u3, w_grp, scale.reshape(1, d), h3)
    return out.reshape(n, d)


def _router_kernel(h_ref, g_ref, whi_ref, wlo_ref, b_ref, eid_ref, wts_ref, rank_ref, cnt_ref, carry_s):
    i = pl.program_id(0)
    tm = h_ref.shape[0]

    @pl.when(i == 0)
    def _():
        carry_s[...] = jnp.zeros_like(carry_s)

    xn = _rms(h_ref[...], g_ref[...])
    x_hi = xn.astype(BF16)
    x_lo = (xn - x_hi.astype(F32)).astype(BF16)
    w_hi = whi_ref[...]
    lt = (jnp.dot(x_hi, w_hi, preferred_element_type=F32)
          + jnp.dot(x_lo, w_hi, preferred_element_type=F32)
          + jnp.dot(x_hi, wlo_ref[...], preferred_element_type=F32))
    logits = lt.T + b_ref[:, 0:1]

    c = [logits[r:r + 1] for r in range(N_EXPERT_GROUPS)]
    cmax = jnp.maximum(jnp.maximum(c[0], c[1]), jnp.maximum(c[2], c[3]))
    grp = jnp.where(c[0] == cmax, 0, jnp.where(c[1] == cmax, 1, jnp.where(c[2] == cmax, 2, 3)))
    csum = (jnp.exp(c[0] - cmax) + jnp.exp(c[1] - cmax)) + (jnp.exp(c[2] - cmax) + jnp.exp(c[3] - cmax))
    p_grp = 1.0 / csum

    epg = EXPERTS_PER_GROUP
    sel = jnp.zeros((epg, tm), F32)
    for gi in range(N_EXPERT_GROUPS):
        blk = logits[FINE_ROW0 + gi * epg:FINE_ROW0 + (gi + 1) * epg]
        sel = jnp.where(grp == gi, blk, sel)
    ridx = lax.broadcasted_iota(I32, (epg, tm), 0)
    m1 = jnp.max(sel, axis=0, keepdims=True)
    i1 = jnp.min(jnp.where(sel == m1, ridx, epg), axis=0, keepdims=True)
    sel2 = jnp.where(ridx == i1, -jnp.inf, sel)
    m2 = jnp.max(sel2, axis=0, keepdims=True)
    i2 = jnp.min(jnp.where(sel2 == m2, ridx, epg), axis=0, keepdims=True)
    ssum = jnp.sum(jnp.exp(sel - m1), axis=0, keepdims=True)
    p1 = 1.0 / ssum
    p2 = jnp.exp(m2 - m1) / ssum
    den = p1 + p2
    e1 = grp * epg + i1
    e2 = grp * epg + i2
    eid_ref[0:1, :] = e1
    eid_ref[1:2, :] = e2
    wts_ref[0:1, :] = p1 / den * p_grp
    wts_ref[1:2, :] = p2 / den * p_grp

    eidx = lax.broadcasted_iota(I32, (N_EXPERTS, tm), 0)
    hit1 = eidx == e1
    hit2 = eidx == e2
    onehot = jnp.where(hit1, 1.0, jnp.where(hit2, 1.0, 0.0))
    tri = jnp.where(lax.broadcasted_iota(I32, (tm, tm), 0) <= lax.broadcasted_iota(I32, (tm, tm), 1),
                    1.0, 0.0).astype(BF16)
    pref = jnp.dot(onehot.astype(BF16), tri, preferred_element_type=F32)
    tot = pref + carry_s[:, 0:1]
    rank_ref[0:1, :] = (jnp.sum(jnp.where(hit1, tot, 0.0), axis=0, keepdims=True) - 1.0).astype(I32)
    rank_ref[1:2, :] = (jnp.sum(jnp.where(hit2, tot, 0.0), axis=0, keepdims=True) - 1.0).astype(I32)
    carry_s[...] = carry_s[...] + pref[:, tm - 1:tm]
    cnt_ref[...] = carry_s[...]


def _router(h, g, w_hi, w_lo, bias):
    n, d = h.shape
    tm = TOK_TILE
    two = pl.BlockSpec((2, tm), lambda i: (0, i))
    return pl.pallas_call(
        _router_kernel,
        grid=(n // tm,),
        in_specs=[pl.BlockSpec((tm, d), lambda i: (i, 0)),
                  pl.BlockSpec((1, d), lambda i: (0, 0)),
                  pl.BlockSpec((d, ROUTER_ROWS), lambda i: (0, 0)),
                  pl.BlockSpec((d, ROUTER_ROWS), lambda i: (0, 0)),
                  pl.BlockSpec((ROUTER_ROWS, LANE), lambda i: (0, 0))],
        out_specs=[two, two, two, pl.BlockSpec((N_EXPERTS, LANE), lambda i: (0, 0))],
        out_shape=[jax.ShapeDtypeStruct((2, n), I32), jax.ShapeDtypeStruct((2, n), F32),
                   jax.ShapeDtypeStruct((2, n), I32), jax.ShapeDtypeStruct((N_EXPERTS, LANE), F32)],
        scratch_shapes=[pltpu.VMEM((N_EXPERTS, LANE), F32)],
        compiler_params=_cparams("arbitrary"),
        name="moe_router",
    )(h, g.reshape(1, d), w_hi, w_lo, bias)


def _router_params(w_coarse, b_coarse, w_fine, b_fine):
    d = w_coarse.shape[0]
    pad0 = jnp.zeros((d, FINE_ROW0 - N_EXPERT_GROUPS), F32)
    pad1 = jnp.zeros((d, ROUTER_ROWS - FINE_ROW0 - N_EXPERTS), F32)
    wr = jnp.concatenate([w_coarse.astype(F32), pad0, w_fine.astype(F32), pad1], axis=1)
    w_hi = wr.astype(BF16)
    w_lo = (wr - w_hi.astype(F32)).astype(BF16)
    bias = jnp.concatenate([b_coarse.astype(F32), pad0[0], b_fine.astype(F32), pad1[0]])
    return w_hi, w_lo, jnp.broadcast_to(bias[:, None], (ROUTER_ROWS, LANE))


def _route_tables(counts, n_tiles):
    cnt = counts[:, 0].astype(I32)
    tiles_e = (cnt + EXP_TILE - 1) // EXP_TILE
    tile_end = jnp.cumsum(tiles_e)
    tile_start = tile_end - tiles_e
    n_used = tile_end[-1]
    tq = jnp.minimum(jnp.arange(n_tiles, dtype=I32), n_used - 1)
    te = jnp.sum((tile_end[None, :] <= tq[:, None]).astype(I32), axis=1)
    onehot = (te[:, None] == jnp.arange(N_EXPERTS, dtype=I32)[None, :]).astype(I32)
    cnt_t = jnp.sum(onehot * cnt[None, :], axis=1)
    start_t = jnp.sum(onehot * tile_start[None, :], axis=1)
    tid = jnp.arange(n_tiles, dtype=I32)
    valid = jnp.where(tid < n_used, jnp.clip(cnt_t - (tid - start_t) * EXP_TILE, 0, EXP_TILE), 0)
    seg_start = jnp.broadcast_to((tile_start * EXP_TILE)[:, None], (N_EXPERTS, LANE))
    pad_start = tile_start * EXP_TILE + cnt
    ar = jnp.arange(N_EXPERTS, dtype=I32)
    live = tiles_e > 0
    order = jnp.cumsum(live.astype(I32)) - live.astype(I32)
    later = jnp.logical_and(ar[None, :] > ar[:, None], live[None, :])
    nxt_e = jnp.min(jnp.where(later, ar[None, :], N_EXPERTS), axis=1)
    nxt_e = jnp.where(nxt_e >= N_EXPERTS, -1, nxt_e)
    slot_t = jnp.sum(onehot * (order & 1)[None, :], axis=1)
    nxt_t = jnp.sum(onehot * nxt_e[None, :], axis=1)
    return (seg_start.astype(I32), pad_start.astype(I32), te.astype(I32), valid.astype(I32),
            jnp.reshape(n_used, (1,)).astype(I32), slot_t.astype(I32), nxt_t.astype(I32))


def _pos_kernel(eid_ref, rank_ref, st_ref, pos_ref):
    nsub, _, tm = pos_ref.shape
    eidx = lax.broadcasted_iota(I32, (N_EXPERTS, tm), 0)
    st = st_ref[:, 0:1]
    for j in range(nsub):
        sl = slice(j * tm, (j + 1) * tm)
        for k in range(2):
            base = jnp.sum(jnp.where(eidx == eid_ref[k:k + 1, sl], st, 0), axis=0, keepdims=True)
            pos_ref[j, k:k + 1, :] = base + rank_ref[k:k + 1, sl]


def _positions(eid, rank, seg_start):
    n = eid.shape[1]
    tm = TOK_TILE
    nsub = 8 if (n // tm) % 8 == 0 else 1
    two = pl.BlockSpec((2, nsub * tm), lambda i: (0, i))
    return pl.pallas_call(
        _pos_kernel,
        grid=(n // (nsub * tm),),
        in_specs=[two, two, pl.BlockSpec((N_EXPERTS, LANE), lambda i: (0, 0))],
        out_specs=pl.BlockSpec((nsub, 2, tm), lambda i: (i, 0, 0)),
        out_shape=jax.ShapeDtypeStruct((n // tm, 2, tm), I32),
        compiler_params=_cparams("parallel"),
        name="moe_positions",
    )(eid, rank, seg_start)


SUB = 8


def _dispatch_kernel(pad_ref, nu_ref, pos_ref, h_ref, g_ref, xs_ref, buf, zbuf, sem, zsem):
    i = pl.program_id(0)
    n = pl.num_programs(0)
    tm = h_ref.shape[0]
    half = h_ref.shape[1] // 2
    slot = i % 2

    def wait_slot(s):
        for _ in range(2):
            pltpu.make_async_copy(zbuf, xs_ref.at[pl.ds(0, tm)], sem.at[s]).wait()

    @pl.when(i == 0)
    def _():
        zbuf[...] = jnp.zeros_like(zbuf)
        zt = zbuf.shape[0]
        sizes = [1 << k for k in range(zt.bit_length() - 1)]

        def pad_pieces(e, act):
            off = pad_ref[e]
            length = (zt - (off & (zt - 1))) & (zt - 1)
            for sz in sizes:
                take = (length & sz) != 0

                @pl.when(take)
                def _(off=off, sz=sz):
                    if sz < SUB:
                        for r in range(sz):
                            act(pltpu.make_async_copy(zbuf.at[pl.ds(0, 1)], xs_ref.at[pl.ds(off + r, 1)], zsem))
                    else:
                        act(pltpu.make_async_copy(zbuf.at[pl.ds(0, sz)],
                                                  xs_ref.at[pl.ds(pl.multiple_of(off, SUB), sz)], zsem))

                off = off + jnp.where(take, sz, 0)

        def tail_piece(j, act):
            act(pltpu.make_async_copy(zbuf, xs_ref.at[pl.ds(pl.multiple_of(j * zt, zt), zt)], zsem))

        n_all = xs_ref.shape[0] // zt
        for act in (lambda c: c.start(), lambda c: c.wait()):
            lax.fori_loop(0, N_EXPERTS, lambda e, carry, act=act: (pad_pieces(e, act), carry)[1], 0)
            lax.fori_loop(nu_ref[0], n_all, lambda j, carry, act=act: (tail_piece(j, act), carry)[1], 0)

    @pl.when(i >= 2)
    def _():
        wait_slot(slot)

    xn = _rms(h_ref[...], g_ref[...])
    buf[slot] = _pack_pairs(xn[:, :half], xn[:, half:]).reshape(tm // SUB, SUB, half)

    def issue(q, carry):
        for s in range(SUB):
            for k in range(2):
                pltpu.make_async_copy(buf.at[slot, q, pl.ds(s, 1)],
                                      xs_ref.at[pl.ds(pos_ref[q, 2 * s + k], 1)],
                                      sem.at[slot]).start(priority=k)
        return carry

    lax.fori_loop(0, tm // SUB, issue, 0)

    @pl.when(i == n - 1)
    def _():
        wait_slot(slot)

        @pl.when(n >= 2)
        def _():
            wait_slot(1 - slot)


def _dispatch(h, g, pos3, pad_start, n_used, n_rows, word_dtype):
    n, d = h.shape
    tm = TOK_TILE
    assert tm == EXP_TILE
    half = d // 2
    grid_spec = pltpu.PrefetchScalarGridSpec(
        num_scalar_prefetch=2,
        grid=(n // tm,),
        in_specs=[pl.BlockSpec((None, tm // SUB, 2 * SUB), lambda i, pad, nu: (i, 0, 0), memory_space=pltpu.SMEM),
                  pl.BlockSpec((tm, d), lambda i, pad, nu: (i, 0)),
                  pl.BlockSpec((1, d), lambda i, pad, nu: (0, 0))],
        out_specs=pl.BlockSpec(memory_space=pl.ANY),
        scratch_shapes=[pltpu.VMEM((2, tm // SUB, SUB, half), word_dtype),
                        pltpu.VMEM((EXP_TILE, half), word_dtype),
                        pltpu.SemaphoreType.DMA((2,)), pltpu.SemaphoreType.DMA(())],
    )
    return pl.pallas_call(
        _dispatch_kernel,
        grid_spec=grid_spec,
        out_shape=jax.ShapeDtypeStruct((n_rows, half), word_dtype),
        compiler_params=_cparams("arbitrary"),
        name="moe_dispatch",
    )(pad_start, n_used, pos3, h, g.reshape(1, d))


def _expert_kernel(te_ref, va_ref, nu_ref, sl_ref, nx_ref, xs_ref, wg_hbm, wu_hbm, wd_hbm, ys_ref,
                   wgf, wuf, wdf, wgb, wub, wdb, wsem, *, layer):
    del nu_ref
    i = pl.program_id(0)
    e = te_ref[i]
    slot = sl_ref[i]
    changed = jnp.logical_or(i == 0, e != te_ref[jnp.maximum(i - 1, 0)])

    def weight_copies(ex, s):
        return [pltpu.make_async_copy(wg_hbm.at[layer, ex], wgf.at[s], wsem.at[s]),
                pltpu.make_async_copy(wu_hbm.at[layer, ex], wuf.at[s], wsem.at[s]),
                pltpu.make_async_copy(wd_hbm.at[layer, ex], wdf.at[s], wsem.at[s])]

    @pl.when(i == 0)
    def _():
        for c in weight_copies(e, slot):
            c.start()

    @pl.when(changed)
    def _():
        for c in weight_copies(e, slot):
            c.wait()
        wgb[...] = wgf[slot].astype(BF16)
        wub[...] = wuf[slot].astype(BF16)
        wdb[...] = wdf[slot].astype(BF16)
        nxt = nx_ref[i]

        @pl.when(nxt >= 0)
        def _():
            for c in weight_copies(nxt, 1 - slot):
                c.start()

    valid = va_ref[i]

    @pl.when(valid > 0)
    def _():
        lo, hi = _unpack_pairs(xs_ref[...])
        lo = lo.astype(BF16)
        hi = hi.astype(BF16)
        half = lo.shape[1]
        gate = (jnp.dot(lo, wgb[0:half, :], preferred_element_type=F32)
                + jnp.dot(hi, wgb[half:2 * half, :], preferred_element_type=F32))
        up = (jnp.dot(lo, wub[0:half, :], preferred_element_type=F32)
              + jnp.dot(hi, wub[half:2 * half, :], preferred_element_type=F32))
        hid = (gate * jax.nn.sigmoid(gate)) * up
        y = jnp.dot(hid.astype(BF16), wdb[...], preferred_element_type=F32)
        ys_ref[...] = _pack_pairs(y[:, :half], y[:, half:])

    @pl.when(valid <= 0)
    def _():
        ys_ref[...] = jnp.zeros_like(ys_ref)


def _experts(xs, te, valid, n_used, slot_t, nxt_t, w_gate, w_up, w_down, layer):
    n_tiles = te.shape[0]
    _, half = xs.shape
    _, _, d, de = w_gate.shape
    te_tile = EXP_TILE
    hbm = pl.BlockSpec(memory_space=pl.ANY)
    grid_spec = pltpu.PrefetchScalarGridSpec(
        num_scalar_prefetch=5,
        grid=(n_tiles,),
        in_specs=[pl.BlockSpec((te_tile, half), lambda i, te, va, nu, sl, nx: (jnp.minimum(i, nu[0] - 1), 0)),
                  hbm, hbm, hbm],
        out_specs=pl.BlockSpec((te_tile, half), lambda i, te, va, nu, sl, nx: (i, 0)),
        scratch_shapes=[pltpu.VMEM((2, d, de), F32), pltpu.VMEM((2, d, de), F32), pltpu.VMEM((2, de, d), F32),
                        pltpu.VMEM((d, de), BF16), pltpu.VMEM((d, de), BF16), pltpu.VMEM((de, d), BF16),
                        pltpu.SemaphoreType.DMA((2,))],
    )
    return pl.pallas_call(
        functools.partial(_expert_kernel, layer=layer),
        grid_spec=grid_spec,
        out_shape=jax.ShapeDtypeStruct((n_tiles * te_tile, half), xs.dtype),
        compiler_params=_cparams("arbitrary"),
        name="moe_experts",
    )(te, valid, n_used, slot_t, nxt_t, xs, w_gate, w_up, w_down)


def _combine_kernel(pos_ref, posn_ref, w_ref, h_ref, *rest, with_norm):
    if with_norm:
        g_ref, ys_ref, o_ref, hn_ref, buf, sem = rest
    else:
        ys_ref, o_ref, buf, sem = rest
    i = pl.program_id(0)
    n = pl.num_programs(0)
    tm = h_ref.shape[0]
    half = h_ref.shape[1] // 2
    slot = i % 2

    def issue(p_ref, s):
        def body(q, carry):
            for sub in range(SUB):
                for k in range(2):
                    pltpu.make_async_copy(ys_ref.at[pl.ds(p_ref[q, 2 * sub + k], 1)],
                                          buf.at[s, k, q, pl.ds(sub, 1)], sem.at[s]).start(priority=k)
            return carry
        lax.fori_loop(0, tm // SUB, body, 0)

    @pl.when(i == 0)
    def _():
        issue(pos_ref, 0)

    @pl.when(i + 1 < n)
    def _():
        issue(posn_ref, 1 - slot)

    for k in range(2):
        for _ in range(tm // SUB):
            pltpu.make_async_copy(ys_ref.at[pl.ds(0, SUB)], buf.at[slot, k, 0], sem.at[slot]).wait()

    lo0, hi0 = _unpack_pairs(buf[slot, 0].reshape(tm, half))
    lo1, hi1 = _unpack_pairs(buf[slot, 1].reshape(tm, half))
    w0 = jnp.broadcast_to(w_ref[0:1, :], (LANE, tm)).T[:, 0:1]
    w1 = jnp.broadcast_to(w_ref[1:2, :], (LANE, tm)).T[:, 0:1]
    h = h_ref[...]
    hnew = jnp.concatenate([h[:, :half] + (w0 * lo0 + w1 * lo1),
                            h[:, half:] + (w0 * hi0 + w1 * hi1)], axis=1)
    o_ref[...] = hnew
    if with_norm:
        hn_ref[...] = _rms(hnew, g_ref[...]).astype(hn_ref.dtype)


def _combine(h, ys, pos3, wts, g_next):
    n, d = h.shape
    tm = TOK_TILE
    nt = n // tm
    half = ys.shape[1]
    with_norm = g_next is not None
    smem = functools.partial(pl.BlockSpec, (None, tm // SUB, 2 * SUB), memory_space=pltpu.SMEM)
    row = pl.BlockSpec((tm, d), lambda i: (i, 0))
    in_specs = [smem(lambda i: (i, 0, 0)),
                smem(lambda i: (jnp.minimum(i + 1, nt - 1), 0, 0)),
                pl.BlockSpec((2, tm), lambda i: (0, i)),
                row]
    args = [pos3, pos3, wts, h]
    out_specs = [row]
    out_shape = [jax.ShapeDtypeStruct((n, d), F32)]
    if with_norm:
        in_specs.append(pl.BlockSpec((1, d), lambda i: (0, 0)))
        args.append(g_next.reshape(1, d))
        out_specs.append(row)
        out_shape.append(jax.ShapeDtypeStruct((n, d), BF16))
    in_specs.append(pl.BlockSpec(memory_space=pl.ANY))
    args.append(ys)
    out = pl.pallas_call(
        functools.partial(_combine_kernel, with_norm=with_norm),
        grid=(nt,),
        in_specs=in_specs,
        out_specs=out_specs,
        out_shape=out_shape,
        scratch_shapes=[pltpu.VMEM((2, 2, tm // SUB, SUB, half), ys.dtype), pltpu.SemaphoreType.DMA((2,))],
        compiler_params=_cparams("arbitrary"),
        name="moe_combine",
    )(*args)
    return (out[0], out[1]) if with_norm else (out[0], None)


def _hier_moe(h, g_ffn, g_next, layer, w_coarse, b_coarse, w_fine, b_fine, w_gate, w_up, w_down):
    n = h.shape[0]
    n_tiles = (2 * n + N_EXPERTS * (EXP_TILE - 1) + EXP_TILE - 1) // EXP_TILE
    w_hi, w_lo, bias = _router_params(w_coarse, b_coarse, w_fine, b_fine)
    eid, wts, rank, counts = _router(h, g_ffn, w_hi, w_lo, bias)
    seg_start, pad_start, te, valid, n_used, slot_t, nxt_t = _route_tables(counts, n_tiles)
    pos3 = _positions(eid, rank, seg_start)
    nt = pos3.shape[0]
    pos3 = pos3.reshape(nt, 2, TOK_TILE // SUB, SUB).transpose(0, 2, 3, 1).reshape(nt, TOK_TILE // SUB, 2 * SUB)
    word_dtype = jax.eval_shape(lambda a: _pack_pairs(a, a), jax.ShapeDtypeStruct((8, LANE), F32)).dtype
    xs = _dispatch(h, g_ffn, pos3, pad_start, n_used, n_tiles * EXP_TILE, word_dtype)
    ys = _experts(xs, te, valid, n_used, slot_t, nxt_t, w_gate, w_up, w_down, layer)
    return _combine(h, ys, pos3, wts, g_next)


def kernel(x, meta_tokens, norm_mix, norm_ffn, norm_final, s5_w_in, s5_a_re, s5_a_im, s5_log_dt,
           s5_b_re, s5_b_im, s5_c_re, s5_c_im, s5_d, s5_w_glu, pool_w_in, pool_w_grp, pool_scale,
           moe_w_coarse, moe_b_coarse, moe_w_fine, moe_b_fine, moe_w_gate, moe_w_up, moe_w_down):
    b, seq, d = x.shape
    depth = norm_mix.shape[0]
    assert meta_tokens.shape[0] == N_META == CHUNK and seq % CHUNK == 0
    assert LANE % b == 0 and b % 16 == 0
    per = LANE // b
    nc = seq // CHUNK
    assert nc % per == 0
    ncp = nc + per
    slp = ncp * b

    h, hn = _ingest(x, meta_tokens, norm_mix[0], ncp)
    for i in range(depth):
        j = i // 2
        if i % 2 == 0:
            g = s5_a_re.shape[1]
            ops = _s5_prep(s5_a_re[j], s5_a_im[j], s5_log_dt[j], s5_b_re[j], s5_b_im[j],
                           s5_c_re[j], s5_c_im[j])
            d_b = jnp.broadcast_to(jnp.tile(s5_d[j].reshape(g, 1, S5_GROUP), (1, CHUNK, 1))
                                   .reshape(g, CHUNK_W, 1), (g, CHUNK_W, LANE))
            ut = _s5_in(hn, s5_w_in[j].astype(BF16), slp)
            zt = _s5_core(ut, ops, d_b, b, nc)
            h = _s5_out(zt, s5_w_glu[j].astype(BF16), h, slp)
        else:
            u = _matmul(hn, pool_w_in[j].astype(BF16), F32)
            h = _pool_core(u, pool_w_grp[j].astype(BF16), pool_scale[j], h, b, slp, nc)
        g_next = None if i == depth - 1 else norm_mix[i + 1]
        h, hn = _hier_moe(h, norm_ffn[i], g_next, i,
                          moe_w_coarse[i], moe_b_coarse[i], moe_w_fine[i], moe_b_fine[i],
                          moe_w_gate, moe_w_up, moe_w_down)
    return _egress(h, norm_final, b, seq, ncp)
```

```python
import functools

import jax
import jax.numpy as jnp
from jax import lax
from jax.experimental import pallas as pl
from jax.experimental.pallas import tpu as pltpu

F32 = jnp.float32
BF16 = jnp.bfloat16
I32 = jnp.int32

N_META = 16
CHUNK = 16
RMS_EPS = 1e-6
S5_GROUP = 16
S5_STATE = 64
CHUNK_W = CHUNK * S5_GROUP
POOL_WINDOWS = (2, 4, 8, 16)
N_EXPERT_GROUPS = 4
EXPERTS_PER_GROUP = 8
N_EXPERTS = N_EXPERT_GROUPS * EXPERTS_PER_GROUP
ROUTER_ROWS = 128
FINE_ROW0 = 8

LANE = 128
TOK_TILE = 256
EXP_TILE = 256
VMEM_LIMIT = 56 * 1024 * 1024

_NT = (((1,), (1,)), ((), ()))
_TN = (((0,), (0,)), ((), ()))


def _cparams(*sem):
    return pltpu.CompilerParams(dimension_semantics=sem, vmem_limit_bytes=VMEM_LIMIT)


def _rms(x, g):
    return x * lax.rsqrt(jnp.mean(x * x, axis=-1, keepdims=True) + RMS_EPS) * g


def _pack_pairs(lo, hi):
    return pltpu.pack_elementwise([lo, hi], packed_dtype=BF16)


def _unpack_pairs(w):
    lo = pltpu.unpack_elementwise(w, index=0, packed_dtype=BF16, unpacked_dtype=F32)
    hi = pltpu.unpack_elementwise(w, index=1, packed_dtype=BF16, unpacked_dtype=F32)
    return lo, hi


def _ingest_kernel(x_ref, meta_ref, g_ref, h_ref, hn_ref, *, nc):
    c = pl.program_id(0)
    g = g_ref[...]

    def emit(t, h):
        h_ref[t] = h
        hn_ref[t] = _rms(h, g).astype(hn_ref.dtype)

    @pl.when(c < nc)
    def _():
        nb, d = h_ref.shape[1], h_ref.shape[2]
        rows = nb * CHUNK
        x = x_ref[...].reshape(rows, d)
        r = lax.broadcasted_iota(I32, (rows, rows), 0)
        q = lax.broadcasted_iota(I32, (rows, rows), 1)
        lb, lt = nb.bit_length() - 1, CHUNK.bit_length() - 1
        hit = jnp.logical_and((r >> lb) == (q & (CHUNK - 1)), (r & (nb - 1)) == (q >> lt))
        perm = jnp.where(hit, 1.0, 0.0).astype(BF16)
        x1 = x.astype(BF16)
        r1 = x - x1.astype(F32)
        x2 = r1.astype(BF16)
        x3 = (r1 - x2.astype(F32)).astype(BF16)
        y = (jnp.dot(perm, x1, preferred_element_type=F32) + jnp.dot(perm, x2, preferred_element_type=F32)
             + jnp.dot(perm, x3, preferred_element_type=F32))
        for t in range(CHUNK):
            emit(t, y[t * nb:(t + 1) * nb, :])

    @pl.when(c == nc)
    def _():
        nb, d = h_ref.shape[1], h_ref.shape[2]
        for t in range(CHUNK):
            emit(t, jnp.broadcast_to(meta_ref[t:t + 1, :], (nb, d)))

    @pl.when(c > nc)
    def _():
        h_ref[...] = jnp.zeros_like(h_ref)
        hn_ref[...] = jnp.zeros_like(hn_ref)


def _ingest(x, meta, g, ncp):
    b, seq, d = x.shape
    nc = seq // CHUNK
    out = pl.BlockSpec((CHUNK, b, d), lambda c: (0, c, 0))
    h3, hn3 = pl.pallas_call(
        functools.partial(_ingest_kernel, nc=nc),
        grid=(ncp,),
        in_specs=[pl.BlockSpec((b, None, CHUNK, d), lambda c: (0, jnp.minimum(c, nc - 1), 0, 0)),
                  pl.BlockSpec((N_META, d), lambda c: (0, 0)),
                  pl.BlockSpec((1, d), lambda c: (0, 0))],
        out_specs=[out, out],
        out_shape=[jax.ShapeDtypeStruct((CHUNK, ncp * b, d), F32),
                   jax.ShapeDtypeStruct((CHUNK, ncp * b, d), BF16)],
        compiler_params=_cparams("parallel"),
        name="ingest",
    )(x.reshape(b, nc, CHUNK, d), meta.astype(F32), g.reshape(1, d))
    n = CHUNK * ncp * b
    return h3.reshape(n, d), hn3.reshape(n, d)


def _egress_kernel(h_ref, g_ref, o_ref):
    g = g_ref[...]
    for t in range(CHUNK):
        o_ref[:, t, :] = _rms(h_ref[t], g)


def _egress(h, g, b, seq, ncp):
    n, d = h.shape
    nc = seq // CHUNK
    out = pl.pallas_call(
        _egress_kernel,
        grid=(nc,),
        in_specs=[pl.BlockSpec((CHUNK, b, d), lambda c: (0, c, 0)),
                  pl.BlockSpec((1, d), lambda c: (0, 0))],
        out_specs=pl.BlockSpec((b, None, CHUNK, d), lambda c: (0, c, 0, 0)),
        out_shape=jax.ShapeDtypeStruct((b, nc, CHUNK, d), F32),
        compiler_params=_cparams("parallel"),
        name="egress",
    )(h.reshape(CHUNK, ncp * b, d), g.reshape(1, d))
    return out.reshape(b, seq, d)


def _mm_kernel(x_ref, w_ref, o_ref):
    o_ref[...] = jnp.dot(x_ref[...], w_ref[...], preferred_element_type=F32).astype(o_ref.dtype)


def _matmul(x, w, out_dtype, tm=512):
    n, k = x.shape
    f = w.shape[1]
    return pl.pallas_call(
        _mm_kernel,
        grid=(n // tm,),
        in_specs=[pl.BlockSpec((tm, k), lambda i: (i, 0)),
                  pl.BlockSpec((k, f), lambda i: (0, 0))],
        out_specs=pl.BlockSpec((tm, f), lambda i: (i, 0)),
        out_shape=jax.ShapeDtypeStruct((n, f), out_dtype),
        compiler_params=_cparams("parallel"),
        name="pool_in",
    )(x, w)


def _mm_t_kernel(x_ref, w_ref, o_ref):
    o_ref[...] = jnp.dot(x_ref[...], w_ref[...], preferred_element_type=F32).astype(o_ref.dtype).T


def _s5_in(hn, w, slp):
    n, k = hn.shape
    f = w.shape[1]
    tf = min(512, f)
    return pl.pallas_call(
        _mm_t_kernel,
        grid=(CHUNK, f // tf),
        in_specs=[pl.BlockSpec((None, slp, k), lambda t, j: (t, 0, 0)),
                  pl.BlockSpec((k, tf), lambda t, j: (0, j))],
        out_specs=pl.BlockSpec((None, tf, slp), lambda t, j: (t, j, 0)),
        out_shape=jax.ShapeDtypeStruct((CHUNK, f, slp), BF16),
        compiler_params=_cparams("parallel", "arbitrary"),
        name="s5_in",
    )(hn.reshape(CHUNK, slp, k), w)


def _s5_prep_kernel(lrc_ref, lic_ref, ldt_ref, brt_ref, bit_ref, cre_ref, cim_ref,
                    lrr_ref, lir_ref, crt_ref, cit_ref,
                    m_ref, wz_ref, wyre_ref, wyim_ref, a16re_ref, a16im_ref):
    g = pl.program_id(0)
    p = S5_STATE
    dt = jnp.exp(ldt_ref[0:1, :])
    lr = lrc_ref[...]
    li = lic_ref[...]
    lam_r = lr * dt
    lam_i = li * dt
    mag = jnp.exp(lam_r)
    abr = mag * jnp.cos(lam_i)
    abi = mag * jnp.sin(lam_i)
    den = lr * lr + li * li
    nr = abr - 1.0
    cfr = (nr * lr + abi * li) / den
    cfi = (abi * lr - nr * li) / den

    def two(x):
        return jnp.concatenate([x, x], axis=1)

    def powers(ar, ai, n):
        out = [(ar, ai)]
        for _ in range(n - 1):
            qr, qi = out[-1]
            out.append((qr * ar - qi * ai, qr * ai + qi * ar))
        return out

    cfr2, cfi2 = two(cfr), two(cfi)
    brt = brt_ref[...]
    bit = bit_ref[...]
    bbr = cfr2 * brt - cfi2 * bit
    bbi = cfr2 * bit + cfi2 * brt
    lane = lax.broadcasted_iota(I32, (1, CHUNK_W), 1)
    t_in = lane >> 4
    colp = powers(abr, abi, CHUNK - 1)
    pr = jnp.where(t_in == CHUNK - 1, 1.0, 0.0) + jnp.zeros_like(bbr)
    pi = jnp.zeros_like(bbr)
    for t in range(CHUNK - 1):
        qr, qi = colp[CHUNK - 2 - t]
        pr = jnp.where(t_in == t, two(qr), pr)
        pi = jnp.where(t_in == t, two(qi), pi)
    wzr = pr * bbr - pi * bbi
    wzi = pr * bbi + pi * bbr
    wz_ref[0:p, :] = wzr.astype(BF16)
    wz_ref[p:2 * p, :] = wzi.astype(BF16)

    krev = (jnp.dot(cre_ref[...], wzr, precision=lax.Precision.HIGHEST, preferred_element_type=F32)
            - jnp.dot(cim_ref[...], wzi, precision=lax.Precision.HIGHEST, preferred_element_type=F32))
    for t in range(CHUNK):
        s = (CHUNK - 1 - t) * S5_GROUP
        blk = krev if s == 0 else pltpu.roll(krev, CHUNK_W - s, 1)
        blk = jnp.where(lane < CHUNK_W - s, blk, 0.0)
        m_ref[t * S5_GROUP:(t + 1) * S5_GROUP, :] = blk.astype(BF16)

    lamr_row = lrr_ref[0:1, :] * dt
    lami_row = lir_ref[0:1, :] * dt
    mag_row = jnp.exp(lamr_row)
    rowp = powers(mag_row * jnp.cos(lami_row), mag_row * jnp.sin(lami_row), CHUNK)
    pwr = jnp.concatenate([jnp.broadcast_to(q[0], (S5_GROUP, LANE)) for q in rowp], axis=0)
    pwi = jnp.concatenate([jnp.broadcast_to(q[1], (S5_GROUP, LANE)) for q in rowp], axis=0)
    cr2 = crt_ref[...]
    ci2 = cit_ref[...]
    lane128 = lax.broadcasted_iota(I32, (1, LANE), 1)
    sel = (lane128 >> 6) == (g % 2)
    wyre_ref[...] = jnp.where(sel, cr2 * pwr - ci2 * pwi, 0.0).astype(BF16)
    wyim_ref[...] = jnp.where(sel, -(cr2 * pwi) - ci2 * pwr, 0.0).astype(BF16)

    a16re_ref[...] = jnp.broadcast_to(rowp[CHUNK - 1][0], (8, LANE))
    a16im_ref[...] = jnp.broadcast_to(rowp[CHUNK - 1][1], (8, LANE))


def _s5_prep(a_re, a_im, log_dt, b_re, b_im, c_re, c_im):
    g, p = a_re.shape
    lrc = jnp.broadcast_to(a_re[:, :, None], (g, p, LANE))
    lic = jnp.broadcast_to(a_im[:, :, None], (g, p, LANE))
    ldt = jnp.broadcast_to(log_dt[:, None, None], (g, 8, LANE))
    brt = jnp.tile(b_re, (1, 1, CHUNK))
    bit = jnp.tile(b_im, (1, 1, CHUNK))
    lrr = jnp.broadcast_to(jnp.concatenate([a_re, a_re], axis=1)[:, None, :], (g, 8, LANE))
    lir = jnp.broadcast_to(jnp.concatenate([a_im, a_im], axis=1)[:, None, :], (g, 8, LANE))
    crt = jnp.tile(jnp.concatenate([c_re, c_re], axis=2), (1, CHUNK, 1))
    cit = jnp.tile(jnp.concatenate([c_im, c_im], axis=2), (1, CHUNK, 1))

    def spec(r, c):
        return pl.BlockSpec((None, r, c), lambda i: (i, 0, 0))

    return pl.pallas_call(
        _s5_prep_kernel,
        grid=(g,),
        in_specs=[spec(p, LANE), spec(p, LANE), spec(8, LANE), spec(p, CHUNK_W), spec(p, CHUNK_W),
                  spec(S5_GROUP, p), spec(S5_GROUP, p), spec(8, LANE), spec(8, LANE),
                  spec(CHUNK_W, LANE), spec(CHUNK_W, LANE)],
        out_specs=[spec(CHUNK_W, CHUNK_W), spec(2 * p, CHUNK_W), spec(CHUNK_W, LANE),
                   spec(CHUNK_W, LANE), spec(8, LANE), spec(8, LANE)],
        out_shape=[jax.ShapeDtypeStruct((g, CHUNK_W, CHUNK_W), BF16),
                   jax.ShapeDtypeStruct((g, 2 * p, CHUNK_W), BF16),
                   jax.ShapeDtypeStruct((g, CHUNK_W, LANE), BF16),
                   jax.ShapeDtypeStruct((g, CHUNK_W, LANE), BF16),
                   jax.ShapeDtypeStruct((g, 8, LANE), F32),
                   jax.ShapeDtypeStruct((g, 8, LANE), F32)],
        compiler_params=_cparams("parallel"),
        name="s5_prep",
    )(lrc, lic, ldt, brt, bit, c_re, c_im, lrr, lir, crt, cit)


def _gelu_tanh(x):
    return x * (0.5 * (1.0 + jnp.tanh(0.7978845608028654 * (x + 0.044715 * (x * x * x)))))


def _s5_core_kernel(ut_ref, m_ref, wz_ref, wyre_ref, wyim_ref, are_ref, aim_ref, d_ref,
                    o_ref, zre_s, zim_s, spre_s, spim_s, *, nb, nc):
    p = S5_STATE
    slp = ut_ref.shape[2]
    vs, zs = [], []
    for q in range(2):
        v = ut_ref[:, q * S5_GROUP:(q + 1) * S5_GROUP, :].reshape(CHUNK_W, slp)
        vs.append(v)
        zs.append(jnp.dot(wz_ref[q], v, preferred_element_type=F32))
    zre_s[...] = jnp.concatenate([zs[0][0:p], zs[1][0:p]], axis=0).T
    zim_s[...] = jnp.concatenate([zs[0][p:2 * p], zs[1][p:2 * p]], axis=0).T

    lane = lax.broadcasted_iota(I32, (1, LANE), 1)
    are = jnp.where(lane < p, are_ref[0, 0:1, :], are_ref[1, 0:1, :])
    aim = jnp.where(lane < p, aim_ref[0, 0:1, :], aim_ref[1, 0:1, :])

    def step(c, carry):
        sr, si = carry
        r0 = pl.multiple_of(c * nb, nb)
        spre_s[pl.ds(r0, nb), :] = sr.astype(BF16)
        spim_s[pl.ds(r0, nb), :] = si.astype(BF16)
        zr = zre_s[pl.ds(r0, nb), :]
        zi = zim_s[pl.ds(r0, nb), :]
        return are * sr - aim * si + zr, are * si + aim * sr + zi

    rm = nc * nb
    spre_s[rm:slp, :] = jnp.zeros((slp - rm, LANE), BF16)
    spim_s[rm:slp, :] = jnp.zeros((slp - rm, LANE), BF16)
    lax.fori_loop(0, nc, step, (zre_s[rm:rm + nb, :], zim_s[rm:rm + nb, :]))

    spre = spre_s[...]
    spim = spim_s[...]
    for q in range(2):
        y_in = jnp.dot(m_ref[q], vs[q], preferred_element_type=F32)
        y_st = (lax.dot_general(wyre_ref[q], spre, _NT, preferred_element_type=F32)
                + lax.dot_general(wyim_ref[q], spim, _NT, preferred_element_type=F32))
        d = d_ref[q]
        for jb in range(slp // LANE):
            sl = slice(jb * LANE, (jb + 1) * LANE)
            y = y_in[:, sl] + y_st[:, sl] + d * vs[q][:, sl].astype(F32)
            z = _gelu_tanh(y).astype(BF16)
            o_ref[:, q * S5_GROUP:(q + 1) * S5_GROUP, sl] = z.reshape(CHUNK, S5_GROUP, LANE)


def _s5_core(ut, ops, d_b, nb, nc):
    m, wz, wyre, wyim, a16re, a16im = ops
    _, f, slp = ut.shape
    g = f // S5_GROUP

    def pair(r, c):
        return pl.BlockSpec((2, r, c), lambda i: (i, 0, 0))

    io_spec = pl.BlockSpec((CHUNK, 2 * S5_GROUP, slp), lambda i: (0, i, 0))
    return pl.pallas_call(
        functools.partial(_s5_core_kernel, nb=nb, nc=nc),
        grid=(g // 2,),
        in_specs=[io_spec, pair(CHUNK_W, CHUNK_W), pair(2 * S5_STATE, CHUNK_W),
                  pair(CHUNK_W, LANE), pair(CHUNK_W, LANE), pair(8, LANE), pair(8, LANE),
                  pair(CHUNK_W, LANE)],
        out_specs=io_spec,
        out_shape=jax.ShapeDtypeStruct(ut.shape, BF16),
        scratch_shapes=[pltpu.VMEM((slp, LANE), F32), pltpu.VMEM((slp, LANE), F32),
                        pltpu.VMEM((slp, LANE), BF16), pltpu.VMEM((slp, LANE), BF16)],
        compiler_params=_cparams("parallel"),
        name="s5_core",
    )(ut, m, wz, wyre, wyim, a16re, a16im, d_b)


def _s5_out_kernel(z_ref, wv_ref, wg_ref, h_ref, o_ref, zt_s):
    @pl.when(pl.program_id(1) == 0)
    def _():
        zt_s[...] = z_ref[...].T

    z = zt_s[...]
    v = jnp.dot(z, wv_ref[...], preferred_element_type=F32)
    gate = jnp.dot(z, wg_ref[...], preferred_element_type=F32)
    o_ref[...] = h_ref[...] + v * jax.nn.sigmoid(gate)


def _s5_out(zt, w_glu, h, slp):
    _, f, _ = zt.shape
    n, d = h.shape
    tn = min(256, d)
    nj = d // tn
    h3 = h.reshape(CHUNK, slp, d)
    out = pl.pallas_call(
        _s5_out_kernel,
        grid=(CHUNK, nj),
        in_specs=[pl.BlockSpec((None, f, slp), lambda t, j: (t, 0, 0)),
                  pl.BlockSpec((f, tn), lambda t, j: (0, j)),
                  pl.BlockSpec((f, tn), lambda t, j: (0, nj + j)),
                  pl.BlockSpec((None, slp, tn), lambda t, j: (t, 0, j))],
        out_specs=pl.BlockSpec((None, slp, tn), lambda t, j: (t, 0, j)),
        out_shape=jax.ShapeDtypeStruct(h3.shape, F32),
        scratch_shapes=[pltpu.VMEM((slp, f), BF16)],
        compiler_params=_cparams("parallel", "arbitrary"),
        name="s5_out",
    )(zt, w_glu, w_glu, h3)
    return out.reshape(n, d)


def _pool_body(win, first, u_ref, halo_ref, w_ref, sc_ref, h_ref, o_ref, nb):
    rows = u_ref.shape[1]
    row = lax.broadcasted_iota(I32, (rows, 1), 0)

    def cur(t):
        return u_ref[t]

    def prev(t):
        halo = jnp.where(first, 0.0, halo_ref[t])
        return jnp.concatenate([halo, u_ref[t, 0:rows - nb, :]], axis=0)

    def at(tau):
        return cur(tau) if tau >= 0 else prev(tau + CHUNK)

    run = at(0)
    for j in range(1, win):
        run = run + at(-j)
    wmat = w_ref[...]
    scale = sc_ref[...]
    for t in range(CHUNK):
        if t > 0:
            run = run + at(t) - at(t - win)
        if t + 1 >= win:
            inv = 1.0 / win
        else:
            inv = jnp.where(jnp.logical_and(first, row < nb), 1.0 / (t + 1), 1.0 / win)
        mixed = run * inv - cur(t)
        y = jnp.dot(mixed.astype(BF16), wmat, preferred_element_type=F32)
        o_ref[t] = h_ref[t] + y * scale


def _pool_kernel(u_ref, halo_ref, w_ref, sc_ref, h_ref, o_ref, *, nb, n_real):
    g = pl.program_id(0)
    first = pl.program_id(1) == n_real
    for gi, win in enumerate(POOL_WINDOWS):
        @pl.when(g == gi)
        def _(win=win):
            _pool_body(win, first, u_ref, halo_ref, w_ref, sc_ref, h_ref, o_ref, nb)


def _pool_core(u, w_grp, scale, h, nb, slp, nc):
    n, d = h.shape
    ng = len(POOL_WINDOWS)
    cw = d // ng
    rt = LANE
    per = rt // nb
    u3 = u.reshape(CHUNK, slp, d)
    h3 = h.reshape(CHUNK, slp, d)
    blk = pl.BlockSpec((CHUNK, rt, cw), lambda g, i: (0, i, g))

    def halo_map(g, i):
        return (0, jnp.where(i == 0, nc, jnp.maximum(i * per - 1, 0)), g)

    out = pl.pallas_call(
        functools.partial(_pool_kernel, nb=nb, n_real=nc // per),
        grid=(ng, slp // rt),
        in_specs=[blk,
                  pl.BlockSpec((CHUNK, nb, cw), halo_map),
                  pl.BlockSpec((None, cw, cw), lambda g, i: (g, 0, 0)),
                  pl.BlockSpec((1, cw), lambda g, i: (0, g)),
                  blk],
        out_specs=blk,
        out_shape=jax.ShapeDtypeStruct(h3.shape, F32),
        compiler_params=_cparams("parallel", "parallel"),
        name="pool_core",
    )(u3, u3, w_grp, scale.reshape(1, d), h3)
    return out.reshape(n, d)


def _router_kernel(h_ref, g_ref, whi_ref, wlo_ref, b_ref, eid_ref, wts_ref, rank_ref, cnt_ref, carry_s):
    i = pl.program_id(0)
    tm = h_ref.shape[0]

    @pl.when(i == 0)
    def _():
        carry_s[...] = jnp.zeros_like(carry_s)

    xn = _rms(h_ref[...], g_ref[...])
    x_hi = xn.astype(BF16)
    x_lo = (xn - x_hi.astype(F32)).astype(BF16)
    w_hi = whi_ref[...]
    lt = (jnp.dot(x_hi, w_hi, preferred_element_type=F32)
          + jnp.dot(x_lo, w_hi, preferred_element_type=F32)
          + jnp.dot(x_hi, wlo_ref[...], preferred_element_type=F32))
    logits = lt.T + b_ref[:, 0:1]

    c = [logits[r:r + 1] for r in range(N_EXPERT_GROUPS)]
    cmax = jnp.maximum(jnp.maximum(c[0], c[1]), jnp.maximum(c[2], c[3]))
    grp = jnp.where(c[0] == cmax, 0, jnp.where(c[1] == cmax, 1, jnp.where(c[2] == cmax, 2, 3)))
    csum = (jnp.exp(c[0] - cmax) + jnp.exp(c[1] - cmax)) + (jnp.exp(c[2] - cmax) + jnp.exp(c[3] - cmax))
    p_grp = 1.0 / csum

    epg = EXPERTS_PER_GROUP
    sel = jnp.zeros((epg, tm), F32)
    for gi in range(N_EXPERT_GROUPS):
        blk = logits[FINE_ROW0 + gi * epg:FINE_ROW0 + (gi + 1) * epg]
        sel = jnp.where(grp == gi, blk, sel)
    ridx = lax.broadcasted_iota(I32, (epg, tm), 0)
    m1 = jnp.max(sel, axis=0, keepdims=True)
    i1 = jnp.min(jnp.where(sel == m1, ridx, epg), axis=0, keepdims=True)
    sel2 = jnp.where(ridx == i1, -jnp.inf, sel)
    m2 = jnp.max(sel2, axis=0, keepdims=True)
    i2 = jnp.min(jnp.where(sel2 == m2, ridx, epg), axis=0, keepdims=True)
    ssum = jnp.sum(jnp.exp(sel - m1), axis=0, keepdims=True)
    p1 = 1.0 / ssum
    p2 = jnp.exp(m2 - m1) / ssum
    den = p1 + p2
    e1 = grp * epg + i1
    e2 = grp * epg + i2
    eid_ref[0:1, :] = e1
    eid_ref[1:2, :] = e2
    wts_ref[0:1, :] = p1 / den * p_grp
    wts_ref[1:2, :] = p2 / den * p_grp

    eidx = lax.broadcasted_iota(I32, (N_EXPERTS, tm), 0)
    hit1 = eidx == e1
    hit2 = eidx == e2
    onehot = jnp.where(hit1, 1.0, jnp.where(hit2, 1.0, 0.0))
    tri = jnp.where(lax.broadcasted_iota(I32, (tm, tm), 0) <= lax.broadcasted_iota(I32, (tm, tm), 1),
                    1.0, 0.0).astype(BF16)
    pref = jnp.dot(onehot.astype(BF16), tri, preferred_element_type=F32)
    tot = pref + carry_s[:, 0:1]
    rank_ref[0:1, :] = (jnp.sum(jnp.where(hit1, tot, 0.0), axis=0, keepdims=True) - 1.0).astype(I32)
    rank_ref[1:2, :] = (jnp.sum(jnp.where(hit2, tot, 0.0), axis=0, keepdims=True) - 1.0).astype(I32)
    carry_s[...] = carry_s[...] + pref[:, tm - 1:tm]
    cnt_ref[...] = carry_s[...]


def _router(h, g, w_hi, w_lo, bias):
    n, d = h.shape
    tm = TOK_TILE
    two = pl.BlockSpec((2, tm), lambda i: (0, i))
    return pl.pallas_call(
        _router_kernel,
        grid=(n // tm,),
        in_specs=[pl.BlockSpec((tm, d), lambda i: (i, 0)),
                  pl.BlockSpec((1, d), lambda i: (0, 0)),
                  pl.BlockSpec((d, ROUTER_ROWS), lambda i: (0, 0)),
                  pl.BlockSpec((d, ROUTER_ROWS), lambda i: (0, 0)),
                  pl.BlockSpec((ROUTER_ROWS, LANE), lambda i: (0, 0))],
        out_specs=[two, two, two, pl.BlockSpec((N_EXPERTS, LANE), lambda i: (0, 0))],
        out_shape=[jax.ShapeDtypeStruct((2, n), I32), jax.ShapeDtypeStruct((2, n), F32),
                   jax.ShapeDtypeStruct((2, n), I32), jax.ShapeDtypeStruct((N_EXPERTS, LANE), F32)],
        scratch_shapes=[pltpu.VMEM((N_EXPERTS, LANE), F32)],
        compiler_params=_cparams("arbitrary"),
        name="moe_router",
    )(h, g.reshape(1, d), w_hi, w_lo, bias)


def _router_params(w_coarse, b_coarse, w_fine, b_fine):
    d = w_coarse.shape[0]
    pad0 = jnp.zeros((d, FINE_ROW0 - N_EXPERT_GROUPS), F32)
    pad1 = jnp.zeros((d, ROUTER_ROWS - FINE_ROW0 - N_EXPERTS), F32)
    wr = jnp.concatenate([w_coarse.astype(F32), pad0, w_fine.astype(F32), pad1], axis=1)
    w_hi = wr.astype(BF16)
    w_lo = (wr - w_hi.astype(F32)).astype(BF16)
    bias = jnp.concatenate([b_coarse.astype(F32), pad0[0], b_fine.astype(F32), pad1[0]])
    return w_hi, w_lo, jnp.broadcast_to(bias[:, None], (ROUTER_ROWS, LANE))


def _route_tables(counts, n_tiles):
    cnt = counts[:, 0].astype(I32)
    tiles_e = (cnt + EXP_TILE - 1) // EXP_TILE
    tile_end = jnp.cumsum(tiles_e)
    tile_start = tile_end - tiles_e
    n_used = tile_end[-1]
    tq = jnp.minimum(jnp.arange(n_tiles, dtype=I32), n_used - 1)
    te = jnp.sum((tile_end[None, :] <= tq[:, None]).astype(I32), axis=1)
    onehot = (te[:, None] == jnp.arange(N_EXPERTS, dtype=I32)[None, :]).astype(I32)
    cnt_t = jnp.sum(onehot * cnt[None, :], axis=1)
    start_t = jnp.sum(onehot * tile_start[None, :], axis=1)
    tid = jnp.arange(n_tiles, dtype=I32)
    valid = jnp.where(tid < n_used, jnp.clip(cnt_t - (tid - start_t) * EXP_TILE, 0, EXP_TILE), 0)
    seg_start = jnp.broadcast_to((tile_start * EXP_TILE)[:, None], (N_EXPERTS, LANE))
    pad_start = tile_start * EXP_TILE + cnt
    ar = jnp.arange(N_EXPERTS, dtype=I32)
    live = tiles_e > 0
    order = jnp.cumsum(live.astype(I32)) - live.astype(I32)
    later = jnp.logical_and(ar[None, :] > ar[:, None], live[None, :])
    nxt_e = jnp.min(jnp.where(later, ar[None, :], N_EXPERTS), axis=1)
    nxt_e = jnp.where(nxt_e >= N_EXPERTS, -1, nxt_e)
    slot_t = jnp.sum(onehot * (order & 1)[None, :], axis=1)
    nxt_t = jnp.sum(onehot * nxt_e[None, :], axis=1)
    return (seg_start.astype(I32), pad_start.astype(I32), te.astype(I32), valid.astype(I32),
            jnp.reshape(n_used, (1,)).astype(I32), slot_t.astype(I32), nxt_t.astype(I32))


def _pos_kernel(eid_ref, rank_ref, st_ref, pos_ref):
    nsub, _, tm = pos_ref.shape
    eidx = lax.broadcasted_iota(I32, (N_EXPERTS, tm), 0)
    st = st_ref[:, 0:1]
    for j in range(nsub):
        sl = slice(j * tm, (j + 1) * tm)
        for k in range(2):
            base = jnp.sum(jnp.where(eidx == eid_ref[k:k + 1, sl], st, 0), axis=0, keepdims=True)
            pos_ref[j, k:k + 1, :] = base + rank_ref[k:k + 1, sl]


def _positions(eid, rank, seg_start):
    n = eid.shape[1]
    tm = TOK_TILE
    nsub = 8 if (n // tm) % 8 == 0 else 1
    two = pl.BlockSpec((2, nsub * tm), lambda i: (0, i))
    return pl.pallas_call(
        _pos_kernel,
        grid=(n // (nsub * tm),),
        in_specs=[two, two, pl.BlockSpec((N_EXPERTS, LANE), lambda i: (0, 0))],
        out_specs=pl.BlockSpec((nsub, 2, tm), lambda i: (i, 0, 0)),
        out_shape=jax.ShapeDtypeStruct((n // tm, 2, tm), I32),
        compiler_params=_cparams("parallel"),
        name="moe_positions",
    )(eid, rank, seg_start)


SUB = 8


def _dispatch_kernel(pad_ref, nu_ref, pos_ref, prev_ref, h_ref, g_ref, xs_ref, buf, zbuf, sem, zsem):
    i = pl.program_id(0)
    n = pl.num_programs(0)
    tm = h_ref.shape[0]
    half = h_ref.shape[1] // 2

    def wait_slot(s):
        for _ in range(2):
            pltpu.make_async_copy(zbuf, xs_ref.at[pl.ds(0, tm)], sem.at[s]).wait()

    @pl.when(i == 0)
    def _():
        zbuf[...] = jnp.zeros_like(zbuf)
        zt = zbuf.shape[0]
        sizes = [1 << k for k in range(zt.bit_length() - 1)]

        def pad_pieces(e, act):
            off = pad_ref[e]
            length = (zt - (off & (zt - 1))) & (zt - 1)
            for sz in sizes:
                take = (length & sz) != 0

                @pl.when(take)
                def _(off=off, sz=sz):
                    if sz < SUB:
                        for r in range(sz):
                            act(pltpu.make_async_copy(zbuf.at[pl.ds(0, 1)], xs_ref.at[pl.ds(off + r, 1)], zsem))
                    else:
                        act(pltpu.make_async_copy(zbuf.at[pl.ds(0, sz)],
                                                  xs_ref.at[pl.ds(pl.multiple_of(off, SUB), sz)], zsem))

                off = off + jnp.where(take, sz, 0)

        def tail_piece(j, act):
            act(pltpu.make_async_copy(zbuf, xs_ref.at[pl.ds(pl.multiple_of(j * zt, zt), zt)], zsem))

        n_all = xs_ref.shape[0] // zt
        for act in (lambda c: c.start(), lambda c: c.wait()):
            lax.fori_loop(0, N_EXPERTS, lambda e, carry, act=act: (pad_pieces(e, act), carry)[1], 0)
            lax.fori_loop(nu_ref[0], n_all, lambda j, carry, act=act: (tail_piece(j, act), carry)[1], 0)

    def pack_tile(s):
        xn = _rms(h_ref[...], g_ref[...])
        buf[s] = _pack_pairs(xn[:, :half], xn[:, half:]).reshape(tm // SUB, SUB, half)

    def row_copy(p_ref, s, q, sub, k):
        return pltpu.make_async_copy(buf.at[s, q, pl.ds(sub, 1)], xs_ref.at[pl.ds(p_ref[q, 2 * sub + k], 1)],
                                     sem.at[s])

    @pl.when(i == 0)
    def _():
        pack_tile(0)

    def step(s):
        @pl.when(i >= 2)
        def _():
            wait_slot(s)

        pack_tile(s)
        for q in range(tm // SUB):
            for sub in range(SUB):
                for k in range(2):
                    row_copy(prev_ref, 1 - s, q, sub, k).start(priority=k)

        @pl.when(i == n - 1)
        def _():
            def body(q, carry):
                for sub in range(SUB):
                    for k in range(2):
                        row_copy(pos_ref, s, q, sub, k).start(priority=k)
                return carry
            lax.fori_loop(0, tm // SUB, body, 0)
            wait_slot(1 - s)
            wait_slot(s)

    for s in range(2):
        pl.when(jnp.logical_and(i >= 1, i % 2 == s))(functools.partial(step, s))


def _dispatch(h, g, pos3, pad_start, n_used, n_rows, word_dtype):
    n, d = h.shape
    tm = TOK_TILE
    assert tm == EXP_TILE
    assert n // tm >= 2
    half = d // 2
    grid_spec = pltpu.PrefetchScalarGridSpec(
        num_scalar_prefetch=2,
        grid=(n // tm,),
        in_specs=[pl.BlockSpec((None, tm // SUB, 2 * SUB), lambda i, pad, nu: (i, 0, 0), memory_space=pltpu.SMEM),
                  pl.BlockSpec((None, tm // SUB, 2 * SUB), lambda i, pad, nu: (jnp.maximum(i - 1, 0), 0, 0),
                               memory_space=pltpu.SMEM),
                  pl.BlockSpec((tm, d), lambda i, pad, nu: (i, 0)),
                  pl.BlockSpec((1, d), lambda i, pad, nu: (0, 0))],
        out_specs=pl.BlockSpec(memory_space=pl.ANY),
        scratch_shapes=[pltpu.VMEM((2, tm // SUB, SUB, half), word_dtype),
                        pltpu.VMEM((EXP_TILE, half), word_dtype),
                        pltpu.SemaphoreType.DMA((2,)), pltpu.SemaphoreType.DMA(())],
    )
    return pl.pallas_call(
        _dispatch_kernel,
        grid_spec=grid_spec,
        out_shape=jax.ShapeDtypeStruct((n_rows, half), word_dtype),
        compiler_params=_cparams("arbitrary"),
        name="moe_dispatch",
    )(pad_start, n_used, pos3, pos3, h, g.reshape(1, d))


def _expert_kernel(te_ref, va_ref, nu_ref, sl_ref, nx_ref, xs_ref, wg_hbm, wu_hbm, wd_hbm, ys_ref,
                   wgf, wuf, wdf, wgb, wub, wdb, wsem, *, layer):
    del nu_ref
    i = pl.program_id(0)
    e = te_ref[i]
    slot = sl_ref[i]
    changed = jnp.logical_or(i == 0, e != te_ref[jnp.maximum(i - 1, 0)])

    def weight_copies(ex, s):
        return [pltpu.make_async_copy(wg_hbm.at[layer, ex], wgf.at[s], wsem.at[s]),
                pltpu.make_async_copy(wu_hbm.at[layer, ex], wuf.at[s], wsem.at[s]),
                pltpu.make_async_copy(wd_hbm.at[layer, ex], wdf.at[s], wsem.at[s])]

    @pl.when(i == 0)
    def _():
        for c in weight_copies(e, slot):
            c.start()

    @pl.when(changed)
    def _():
        for c in weight_copies(e, slot):
            c.wait()
        wgb[...] = wgf[slot].astype(BF16)
        wub[...] = wuf[slot].astype(BF16)
        wdb[...] = wdf[slot].astype(BF16)
        nxt = nx_ref[i]

        @pl.when(nxt >= 0)
        def _():
            for c in weight_copies(nxt, 1 - slot):
                c.start()

    valid = va_ref[i]

    @pl.when(valid > 0)
    def _():
        lo, hi = _unpack_pairs(xs_ref[...])
        lo = lo.astype(BF16)
        hi = hi.astype(BF16)
        half = lo.shape[1]
        gate = (jnp.dot(lo, wgb[0:half, :], preferred_element_type=F32)
                + jnp.dot(hi, wgb[half:2 * half, :], preferred_element_type=F32))
        up = (jnp.dot(lo, wub[0:half, :], preferred_element_type=F32)
              + jnp.dot(hi, wub[half:2 * half, :], preferred_element_type=F32))
        hid = (gate * jax.nn.sigmoid(gate)) * up
        y = jnp.dot(hid.astype(BF16), wdb[...], preferred_element_type=F32)
        ys_ref[...] = _pack_pairs(y[:, :half], y[:, half:])

    @pl.when(valid <= 0)
    def _():
        ys_ref[...] = jnp.zeros_like(ys_ref)


def _experts(xs, te, valid, n_used, slot_t, nxt_t, w_gate, w_up, w_down, layer):
    n_tiles = te.shape[0]
    _, half = xs.shape
    _, _, d, de = w_gate.shape
    te_tile = EXP_TILE
    hbm = pl.BlockSpec(memory_space=pl.ANY)
    grid_spec = pltpu.PrefetchScalarGridSpec(
        num_scalar_prefetch=5,
        grid=(n_tiles,),
        in_specs=[pl.BlockSpec((te_tile, half), lambda i, te, va, nu, sl, nx: (jnp.minimum(i, nu[0] - 1), 0)),
                  hbm, hbm, hbm],
        out_specs=pl.BlockSpec((te_tile, half), lambda i, te, va, nu, sl, nx: (i, 0)),
        scratch_shapes=[pltpu.VMEM((2, d, de), F32), pltpu.VMEM((2, d, de), F32), pltpu.VMEM((2, de, d), F32),
                        pltpu.VMEM((d, de), BF16), pltpu.VMEM((d, de), BF16), pltpu.VMEM((de, d), BF16),
                        pltpu.SemaphoreType.DMA((2,))],
    )
    return pl.pallas_call(
        functools.partial(_expert_kernel, layer=layer),
        grid_spec=grid_spec,
        out_shape=jax.ShapeDtypeStruct((n_tiles * te_tile, half), xs.dtype),
        compiler_params=_cparams("arbitrary"),
        name="moe_experts",
    )(te, valid, n_used, slot_t, nxt_t, xs, w_gate, w_up, w_down)


def _combine_kernel(pos_ref, posn_ref, w_ref, h_ref, *rest, with_norm):
    if with_norm:
        g_ref, ys_ref, o_ref, hn_ref, buf, sem = rest
    else:
        ys_ref, o_ref, buf, sem = rest
    i = pl.program_id(0)
    n = pl.num_programs(0)
    tm = h_ref.shape[0]
    half = h_ref.shape[1] // 2

    def row_copy(p_ref, s, q, sub, k):
        return pltpu.make_async_copy(ys_ref.at[pl.ds(p_ref[q, 2 * sub + k], 1)],
                                     buf.at[s, k, q, pl.ds(sub, 1)], sem.at[s])

    def wait_rows(s):
        for k in range(2):
            for _ in range(tm // SUB):
                pltpu.make_async_copy(ys_ref.at[pl.ds(0, SUB)], buf.at[s, k, 0], sem.at[s]).wait()

    @pl.when(i == 0)
    def _():
        def body(q, carry):
            for sub in range(SUB):
                for k in range(2):
                    row_copy(pos_ref, 0, q, sub, k).start(priority=k)
            return carry
        lax.fori_loop(0, tm // SUB, body, 0)

    def step(s):
        wait_rows(s)
        for q in range(tm // SUB):
            for sub in range(SUB):
                for k in range(2):
                    row_copy(posn_ref, 1 - s, q, sub, k).start(priority=k)
        lo0, hi0 = _unpack_pairs(buf[s, 0].reshape(tm, half))
        lo1, hi1 = _unpack_pairs(buf[s, 1].reshape(tm, half))
        w0 = jnp.broadcast_to(w_ref[0:1, :], (LANE, tm)).T[:, 0:1]
        w1 = jnp.broadcast_to(w_ref[1:2, :], (LANE, tm)).T[:, 0:1]
        h = h_ref[...]
        hnew = jnp.concatenate([h[:, :half] + (w0 * lo0 + w1 * lo1),
                                h[:, half:] + (w0 * hi0 + w1 * hi1)], axis=1)
        o_ref[...] = hnew
        if with_norm:
            hn_ref[...] = _rms(hnew, g_ref[...]).astype(hn_ref.dtype)

        @pl.when(i == n - 1)
        def _():
            wait_rows(1 - s)

    for s in range(2):
        pl.when(i % 2 == s)(functools.partial(step, s))


def _combine(h, ys, pos3, wts, g_next):
    n, d = h.shape
    tm = TOK_TILE
    nt = n // tm
    half = ys.shape[1]
    with_norm = g_next is not None
    smem = functools.partial(pl.BlockSpec, (None, tm // SUB, 2 * SUB), memory_space=pltpu.SMEM)
    row = pl.BlockSpec((tm, d), lambda i: (i, 0))
    in_specs = [smem(lambda i: (i, 0, 0)),
                smem(lambda i: (jnp.minimum(i + 1, nt - 1), 0, 0)),
                pl.BlockSpec((2, tm), lambda i: (0, i)),
                row]
    args = [pos3, pos3, wts, h]
    out_specs = [row]
    out_shape = [jax.ShapeDtypeStruct((n, d), F32)]
    if with_norm:
        in_specs.append(pl.BlockSpec((1, d), lambda i: (0, 0)))
        args.append(g_next.reshape(1, d))
        out_specs.append(row)
        out_shape.append(jax.ShapeDtypeStruct((n, d), BF16))
    in_specs.append(pl.BlockSpec(memory_space=pl.ANY))
    args.append(ys)
    out = pl.pallas_call(
        functools.partial(_combine_kernel, with_norm=with_norm),
        grid=(nt,),
        in_specs=in_specs,
        out_specs=out_specs,
        out_shape=out_shape,
        scratch_shapes=[pltpu.VMEM((2, 2, tm // SUB, SUB, half), ys.dtype), pltpu.SemaphoreType.DMA((2,))],
        compiler_params=_cparams("arbitrary"),
        name="moe_combine",
    )(*args)
    return (out[0], out[1]) if with_norm else (out[0], None)


def _hier_moe(h, g_ffn, g_next, layer, w_coarse, b_coarse, w_fine, b_fine, w_gate, w_up, w_down):
    n = h.shape[0]
    n_tiles = (2 * n + N_EXPERTS * (EXP_TILE - 1) + EXP_TILE - 1) // EXP_TILE
    w_hi, w_lo, bias = _router_params(w_coarse, b_coarse, w_fine, b_fine)
    eid, wts, rank, counts = _router(h, g_ffn, w_hi, w_lo, bias)
    seg_start, pad_start, te, valid, n_used, slot_t, nxt_t = _route_tables(counts, n_tiles)
    pos3 = _positions(eid, rank, seg_start)
    nt = pos3.shape[0]
    pos3 = pos3.reshape(nt, 2, TOK_TILE // SUB, SUB).transpose(0, 2, 3, 1).reshape(nt, TOK_TILE // SUB, 2 * SUB)
    word_dtype = jax.eval_shape(lambda a: _pack_pairs(a, a), jax.ShapeDtypeStruct((8, LANE), F32)).dtype
    xs = _dispatch(h, g_ffn, pos3, pad_start, n_used, n_tiles * EXP_TILE, word_dtype)
    ys = _experts(xs, te, valid, n_used, slot_t, nxt_t, w_gate, w_up, w_down, layer)
    return _combine(h, ys, pos3, wts, g_next)


def kernel(x, meta_tokens, norm_mix, norm_ffn, norm_final, s5_w_in, s5_a_re, s5_a_im, s5_log_dt,
           s5_b_re, s5_b_im, s5_c_re, s5_c_im, s5_d, s5_w_glu, pool_w_in, pool_w_grp, pool_scale,
           moe_w_coarse, moe_b_coarse, moe_w_fine, moe_b_fine, moe_w_gate, moe_w_up, moe_w_down):
    b, seq, d = x.shape
    depth = norm_mix.shape[0]
    assert meta_tokens.shape[0] == N_META == CHUNK and seq % CHUNK == 0
    assert LANE % b == 0 and b % 16 == 0
    per = LANE // b
    nc = seq // CHUNK
    assert nc % per == 0
    ncp = nc + per
    slp = ncp * b

    h, hn = _ingest(x, meta_tokens, norm_mix[0], ncp)
    for i in range(depth):
        j = i // 2
        if i % 2 == 0:
            g = s5_a_re.shape[1]
            ops = _s5_prep(s5_a_re[j], s5_a_im[j], s5_log_dt[j], s5_b_re[j], s5_b_im[j],
                           s5_c_re[j], s5_c_im[j])
            d_b = jnp.broadcast_to(jnp.tile(s5_d[j].reshape(g, 1, S5_GROUP), (1, CHUNK, 1))
                                   .reshape(g, CHUNK_W, 1), (g, CHUNK_W, LANE))
            ut = _s5_in(hn, s5_w_in[j].astype(BF16), slp)
            zt = _s5_core(ut, ops, d_b, b, nc)
            h = _s5_out(zt, s5_w_glu[j].astype(BF16), h, slp)
        else:
            u = _matmul(hn, pool_w_in[j].astype(BF16), F32)
            h = _pool_core(u, pool_w_grp[j].astype(BF16), pool_scale[j], h, b, slp, nc)
        g_next = None if i == depth - 1 else norm_mix[i + 1]
        h, hn = _hier_moe(h, norm_ffn[i], g_next, i,
                          moe_w_coarse[i], moe_b_coarse[i], moe_w_fine[i], moe_b_fine[i],
                          moe_w_gate, moe_w_up, moe_w_down)
    return _egress(h, norm_final, b, seq, ncp)
```

```python
import functools

import jax
import jax.numpy as jnp
from jax import lax
from jax.experimental import pallas as pl
from jax.experimental.pallas import tpu as pltpu

F32 = jnp.float32
BF16 = jnp.bfloat16
I32 = jnp.int32

N_META = 16
CHUNK = 16
RMS_EPS = 1e-6
S5_GROUP = 16
S5_STATE = 64
CHUNK_W = CHUNK * S5_GROUP
POOL_WINDOWS = (2, 4, 8, 16)
N_EXPERT_GROUPS = 4
EXPERTS_PER_GROUP = 8
N_EXPERTS = N_EXPERT_GROUPS * EXPERTS_PER_GROUP
ROUTER_ROWS = 128
FINE_ROW0 = 8

LANE = 128
TOK_TILE = 256
EXP_TILE = 256
VMEM_LIMIT = 56 * 1024 * 1024

_NT = (((1,), (1,)), ((), ()))
_TN = (((0,), (0,)), ((), ()))


def _cparams(*sem):
    return pltpu.CompilerParams(dimension_semantics=sem, vmem_limit_bytes=VMEM_LIMIT)


def _rms(x, g):
    return x * lax.rsqrt(jnp.mean(x * x, axis=-1, keepdims=True) + RMS_EPS) * g


def _pack_pairs(lo, hi):
    return pltpu.pack_elementwise([lo, hi], packed_dtype=BF16)


def _unpack_pairs(w):
    lo = pltpu.unpack_elementwise(w, index=0, packed_dtype=BF16, unpacked_dtype=F32)
    hi = pltpu.unpack_elementwise(w, index=1, packed_dtype=BF16, unpacked_dtype=F32)
    return lo, hi


def _ingest_kernel(x_ref, meta_ref, g_ref, h_ref, hn_ref, *, nc):
    c = pl.program_id(0)
    g = g_ref[...]

    def emit(t, h):
        h_ref[t] = h
        hn_ref[t] = _rms(h, g).astype(hn_ref.dtype)

    @pl.when(c < nc)
    def _():
        nb, d = h_ref.shape[1], h_ref.shape[2]
        rows = nb * CHUNK
        x = x_ref[...].reshape(rows, d)
        r = lax.broadcasted_iota(I32, (rows, rows), 0)
        q = lax.broadcasted_iota(I32, (rows, rows), 1)
        lb, lt = nb.bit_length() - 1, CHUNK.bit_length() - 1
        hit = jnp.logical_and((r >> lb) == (q & (CHUNK - 1)), (r & (nb - 1)) == (q >> lt))
        perm = jnp.where(hit, 1.0, 0.0).astype(BF16)
        x1 = x.astype(BF16)
        r1 = x - x1.astype(F32)
        x2 = r1.astype(BF16)
        x3 = (r1 - x2.astype(F32)).astype(BF16)
        y = (jnp.dot(perm, x1, preferred_element_type=F32) + jnp.dot(perm, x2, preferred_element_type=F32)
             + jnp.dot(perm, x3, preferred_element_type=F32))
        for t in range(CHUNK):
            emit(t, y[t * nb:(t + 1) * nb, :])

    @pl.when(c == nc)
    def _():
        nb, d = h_ref.shape[1], h_ref.shape[2]
        for t in range(CHUNK):
            emit(t, jnp.broadcast_to(meta_ref[t:t + 1, :], (nb, d)))

    @pl.when(c > nc)
    def _():
        h_ref[...] = jnp.zeros_like(h_ref)
        hn_ref[...] = jnp.zeros_like(hn_ref)


def _ingest(x, meta, g, ncp):
    b, seq, d = x.shape
    nc = seq // CHUNK
    out = pl.BlockSpec((CHUNK, b, d), lambda c: (0, c, 0))
    h3, hn3 = pl.pallas_call(
        functools.partial(_ingest_kernel, nc=nc),
        grid=(ncp,),
        in_specs=[pl.BlockSpec((b, None, CHUNK, d), lambda c: (0, jnp.minimum(c, nc - 1), 0, 0)),
                  pl.BlockSpec((N_META, d), lambda c: (0, 0)),
                  pl.BlockSpec((1, d), lambda c: (0, 0))],
        out_specs=[out, out],
        out_shape=[jax.ShapeDtypeStruct((CHUNK, ncp * b, d), F32),
                   jax.ShapeDtypeStruct((CHUNK, ncp * b, d), BF16)],
        compiler_params=_cparams("parallel"),
        name="ingest",
    )(x.reshape(b, nc, CHUNK, d), meta.astype(F32), g.reshape(1, d))
    n = CHUNK * ncp * b
    return h3.reshape(n, d), hn3.reshape(n, d)


def _egress_kernel(h_ref, g_ref, o_ref):
    g = g_ref[...]
    for t in range(CHUNK):
        o_ref[:, t, :] = _rms(h_ref[t], g)


def _egress(h, g, b, seq, ncp):
    n, d = h.shape
    nc = seq // CHUNK
    out = pl.pallas_call(
        _egress_kernel,
        grid=(nc,),
        in_specs=[pl.BlockSpec((CHUNK, b, d), lambda c: (0, c, 0)),
                  pl.BlockSpec((1, d), lambda c: (0, 0))],
        out_specs=pl.BlockSpec((b, None, CHUNK, d), lambda c: (0, c, 0, 0)),
        out_shape=jax.ShapeDtypeStruct((b, nc, CHUNK, d), F32),
        compiler_params=_cparams("parallel"),
        name="egress",
    )(h.reshape(CHUNK, ncp * b, d), g.reshape(1, d))
    return out.reshape(b, seq, d)


def _mm_t_kernel(x_ref, w_ref, o_ref):
    o_ref[...] = jnp.dot(x_ref[...], w_ref[...], preferred_element_type=F32).astype(o_ref.dtype).T


def _s5_in(hn, w, slp):
    n, k = hn.shape
    f = w.shape[1]
    tf = min(512, f)
    return pl.pallas_call(
        _mm_t_kernel,
        grid=(CHUNK, f // tf),
        in_specs=[pl.BlockSpec((None, slp, k), lambda t, j: (t, 0, 0)),
                  pl.BlockSpec((k, tf), lambda t, j: (0, j))],
        out_specs=pl.BlockSpec((None, tf, slp), lambda t, j: (t, j, 0)),
        out_shape=jax.ShapeDtypeStruct((CHUNK, f, slp), BF16),
        compiler_params=_cparams("parallel", "arbitrary"),
        name="s5_in",
    )(hn.reshape(CHUNK, slp, k), w)


def _s5_prep_kernel(lrc_ref, lic_ref, ldt_ref, brt_ref, bit_ref, cre_ref, cim_ref,
                    lrr_ref, lir_ref, crt_ref, cit_ref,
                    m_ref, wz_ref, wyre_ref, wyim_ref, a16re_ref, a16im_ref):
    g = pl.program_id(0)
    p = S5_STATE
    dt = jnp.exp(ldt_ref[0:1, :])
    lr = lrc_ref[...]
    li = lic_ref[...]
    lam_r = lr * dt
    lam_i = li * dt
    mag = jnp.exp(lam_r)
    abr = mag * jnp.cos(lam_i)
    abi = mag * jnp.sin(lam_i)
    den = lr * lr + li * li
    nr = abr - 1.0
    cfr = (nr * lr + abi * li) / den
    cfi = (abi * lr - nr * li) / den

    def two(x):
        return jnp.concatenate([x, x], axis=1)

    def powers(ar, ai, n):
        out = [(ar, ai)]
        for _ in range(n - 1):
            qr, qi = out[-1]
            out.append((qr * ar - qi * ai, qr * ai + qi * ar))
        return out

    cfr2, cfi2 = two(cfr), two(cfi)
    brt = brt_ref[...]
    bit = bit_ref[...]
    bbr = cfr2 * brt - cfi2 * bit
    bbi = cfr2 * bit + cfi2 * brt
    lane = lax.broadcasted_iota(I32, (1, CHUNK_W), 1)
    t_in = lane >> 4
    colp = powers(abr, abi, CHUNK - 1)
    pr = jnp.where(t_in == CHUNK - 1, 1.0, 0.0) + jnp.zeros_like(bbr)
    pi = jnp.zeros_like(bbr)
    for t in range(CHUNK - 1):
        qr, qi = colp[CHUNK - 2 - t]
        pr = jnp.where(t_in == t, two(qr), pr)
        pi = jnp.where(t_in == t, two(qi), pi)
    wzr = pr * bbr - pi * bbi
    wzi = pr * bbi + pi * bbr
    wz_ref[0:p, :] = wzr.astype(BF16)
    wz_ref[p:2 * p, :] = wzi.astype(BF16)

    krev = (jnp.dot(cre_ref[...], wzr, precision=lax.Precision.HIGHEST, preferred_element_type=F32)
            - jnp.dot(cim_ref[...], wzi, precision=lax.Precision.HIGHEST, preferred_element_type=F32))
    for t in range(CHUNK):
        s = (CHUNK - 1 - t) * S5_GROUP
        blk = krev if s == 0 else pltpu.roll(krev, CHUNK_W - s, 1)
        blk = jnp.where(lane < CHUNK_W - s, blk, 0.0)
        m_ref[t * S5_GROUP:(t + 1) * S5_GROUP, :] = blk.astype(BF16)

    lamr_row = lrr_ref[0:1, :] * dt
    lami_row = lir_ref[0:1, :] * dt
    mag_row = jnp.exp(lamr_row)
    rowp = powers(mag_row * jnp.cos(lami_row), mag_row * jnp.sin(lami_row), CHUNK)
    pwr = jnp.concatenate([jnp.broadcast_to(q[0], (S5_GROUP, LANE)) for q in rowp], axis=0)
    pwi = jnp.concatenate([jnp.broadcast_to(q[1], (S5_GROUP, LANE)) for q in rowp], axis=0)
    cr2 = crt_ref[...]
    ci2 = cit_ref[...]
    lane128 = lax.broadcasted_iota(I32, (1, LANE), 1)
    sel = (lane128 >> 6) == (g % 2)
    wyre_ref[...] = jnp.where(sel, cr2 * pwr - ci2 * pwi, 0.0).astype(BF16)
    wyim_ref[...] = jnp.where(sel, -(cr2 * pwi) - ci2 * pwr, 0.0).astype(BF16)

    a16re_ref[...] = jnp.broadcast_to(rowp[CHUNK - 1][0], (8, LANE))
    a16im_ref[...] = jnp.broadcast_to(rowp[CHUNK - 1][1], (8, LANE))


def _s5_prep(a_re, a_im, log_dt, b_re, b_im, c_re, c_im):
    g, p = a_re.shape
    lrc = jnp.broadcast_to(a_re[:, :, None], (g, p, LANE))
    lic = jnp.broadcast_to(a_im[:, :, None], (g, p, LANE))
    ldt = jnp.broadcast_to(log_dt[:, None, None], (g, 8, LANE))
    brt = jnp.tile(b_re, (1, 1, CHUNK))
    bit = jnp.tile(b_im, (1, 1, CHUNK))
    lrr = jnp.broadcast_to(jnp.concatenate([a_re, a_re], axis=1)[:, None, :], (g, 8, LANE))
    lir = jnp.broadcast_to(jnp.concatenate([a_im, a_im], axis=1)[:, None, :], (g, 8, LANE))
    crt = jnp.tile(jnp.concatenate([c_re, c_re], axis=2), (1, CHUNK, 1))
    cit = jnp.tile(jnp.concatenate([c_im, c_im], axis=2), (1, CHUNK, 1))

    def spec(r, c):
        return pl.BlockSpec((None, r, c), lambda i: (i, 0, 0))

    return pl.pallas_call(
        _s5_prep_kernel,
        grid=(g,),
        in_specs=[spec(p, LANE), spec(p, LANE), spec(8, LANE), spec(p, CHUNK_W), spec(p, CHUNK_W),
                  spec(S5_GROUP, p), spec(S5_GROUP, p), spec(8, LANE), spec(8, LANE),
                  spec(CHUNK_W, LANE), spec(CHUNK_W, LANE)],
        out_specs=[spec(CHUNK_W, CHUNK_W), spec(2 * p, CHUNK_W), spec(CHUNK_W, LANE),
                   spec(CHUNK_W, LANE), spec(8, LANE), spec(8, LANE)],
        out_shape=[jax.ShapeDtypeStruct((g, CHUNK_W, CHUNK_W), BF16),
                   jax.ShapeDtypeStruct((g, 2 * p, CHUNK_W), BF16),
                   jax.ShapeDtypeStruct((g, CHUNK_W, LANE), BF16),
                   jax.ShapeDtypeStruct((g, CHUNK_W, LANE), BF16),
                   jax.ShapeDtypeStruct((g, 8, LANE), F32),
                   jax.ShapeDtypeStruct((g, 8, LANE), F32)],
        compiler_params=_cparams("parallel"),
        name="s5_prep",
    )(lrc, lic, ldt, brt, bit, c_re, c_im, lrr, lir, crt, cit)


def _gelu_tanh(x):
    return x * (0.5 * (1.0 + jnp.tanh(0.7978845608028654 * (x + 0.044715 * (x * x * x)))))


def _s5_core_kernel(ut_ref, m_ref, wz_ref, wyre_ref, wyim_ref, are_ref, aim_ref, d_ref,
                    o_ref, zre_s, zim_s, spre_s, spim_s, *, nb, nc):
    p = S5_STATE
    slp = ut_ref.shape[2]
    vs, zs = [], []
    for q in range(2):
        v = ut_ref[:, q * S5_GROUP:(q + 1) * S5_GROUP, :].reshape(CHUNK_W, slp)
        vs.append(v)
        zs.append(jnp.dot(wz_ref[q], v, preferred_element_type=F32))
    zre_s[...] = jnp.concatenate([zs[0][0:p], zs[1][0:p]], axis=0).T
    zim_s[...] = jnp.concatenate([zs[0][p:2 * p], zs[1][p:2 * p]], axis=0).T

    lane = lax.broadcasted_iota(I32, (1, LANE), 1)
    are = jnp.where(lane < p, are_ref[0, 0:1, :], are_ref[1, 0:1, :])
    aim = jnp.where(lane < p, aim_ref[0, 0:1, :], aim_ref[1, 0:1, :])

    def step(c, carry):
        sr, si = carry
        r0 = pl.multiple_of(c * nb, nb)
        spre_s[pl.ds(r0, nb), :] = sr.astype(BF16)
        spim_s[pl.ds(r0, nb), :] = si.astype(BF16)
        zr = zre_s[pl.ds(r0, nb), :]
        zi = zim_s[pl.ds(r0, nb), :]
        return are * sr - aim * si + zr, are * si + aim * sr + zi

    rm = nc * nb
    spre_s[rm:slp, :] = jnp.zeros((slp - rm, LANE), BF16)
    spim_s[rm:slp, :] = jnp.zeros((slp - rm, LANE), BF16)
    lax.fori_loop(0, nc, step, (zre_s[rm:rm + nb, :], zim_s[rm:rm + nb, :]))

    spre = spre_s[...]
    spim = spim_s[...]
    for q in range(2):
        y_in = jnp.dot(m_ref[q], vs[q], preferred_element_type=F32)
        y_st = (lax.dot_general(wyre_ref[q], spre, _NT, preferred_element_type=F32)
                + lax.dot_general(wyim_ref[q], spim, _NT, preferred_element_type=F32))
        d = d_ref[q]
        for jb in range(slp // LANE):
            sl = slice(jb * LANE, (jb + 1) * LANE)
            y = y_in[:, sl] + y_st[:, sl] + d * vs[q][:, sl].astype(F32)
            z = _gelu_tanh(y).astype(BF16)
            o_ref[:, q * S5_GROUP:(q + 1) * S5_GROUP, sl] = z.reshape(CHUNK, S5_GROUP, LANE)


def _s5_core(ut, ops, d_b, nb, nc):
    m, wz, wyre, wyim, a16re, a16im = ops
    _, f, slp = ut.shape
    g = f // S5_GROUP

    def pair(r, c):
        return pl.BlockSpec((2, r, c), lambda i: (i, 0, 0))

    io_spec = pl.BlockSpec((CHUNK, 2 * S5_GROUP, slp), lambda i: (0, i, 0))
    return pl.pallas_call(
        functools.partial(_s5_core_kernel, nb=nb, nc=nc),
        grid=(g // 2,),
        in_specs=[io_spec, pair(CHUNK_W, CHUNK_W), pair(2 * S5_STATE, CHUNK_W),
                  pair(CHUNK_W, LANE), pair(CHUNK_W, LANE), pair(8, LANE), pair(8, LANE),
                  pair(CHUNK_W, LANE)],
        out_specs=io_spec,
        out_shape=jax.ShapeDtypeStruct(ut.shape, BF16),
        scratch_shapes=[pltpu.VMEM((slp, LANE), F32), pltpu.VMEM((slp, LANE), F32),
                        pltpu.VMEM((slp, LANE), BF16), pltpu.VMEM((slp, LANE), BF16)],
        compiler_params=_cparams("parallel"),
        name="s5_core",
    )(ut, m, wz, wyre, wyim, a16re, a16im, d_b)


def _s5_out_kernel(z_ref, wv_ref, wg_ref, h_ref, o_ref, zt_s):
    @pl.when(pl.program_id(1) == 0)
    def _():
        zt_s[...] = z_ref[...].T

    z = zt_s[...]
    v = jnp.dot(z, wv_ref[...], preferred_element_type=F32)
    gate = jnp.dot(z, wg_ref[...], preferred_element_type=F32)
    o_ref[...] = h_ref[...] + v * jax.nn.sigmoid(gate)


def _s5_out(zt, w_glu, h, slp):
    _, f, _ = zt.shape
    n, d = h.shape
    tn = min(256, d)
    nj = d // tn
    h3 = h.reshape(CHUNK, slp, d)
    out = pl.pallas_call(
        _s5_out_kernel,
        grid=(CHUNK, nj),
        in_specs=[pl.BlockSpec((None, f, slp), lambda t, j: (t, 0, 0)),
                  pl.BlockSpec((f, tn), lambda t, j: (0, j)),
                  pl.BlockSpec((f, tn), lambda t, j: (0, nj + j)),
                  pl.BlockSpec((None, slp, tn), lambda t, j: (t, 0, j))],
        out_specs=pl.BlockSpec((None, slp, tn), lambda t, j: (t, 0, j)),
        out_shape=jax.ShapeDtypeStruct(h3.shape, F32),
        scratch_shapes=[pltpu.VMEM((slp, f), BF16)],
        compiler_params=_cparams("parallel", "arbitrary"),
        name="s5_out",
    )(zt, w_glu, w_glu, h3)
    return out.reshape(n, d)


def _pool_body(win, first, u_ref, halo_ref, w_ref, sc_ref, h_ref, o_ref, nb):
    rows = u_ref.shape[1]
    row = lax.broadcasted_iota(I32, (rows, 1), 0)

    def cur(t):
        return u_ref[t]

    def prev(t):
        halo = jnp.where(first, 0.0, halo_ref[t])
        return jnp.concatenate([halo, u_ref[t, 0:rows - nb, :]], axis=0)

    def at(tau):
        return cur(tau) if tau >= 0 else prev(tau + CHUNK)

    run = at(0)
    for j in range(1, win):
        run = run + at(-j)
    wmat = w_ref[...]
    scale = sc_ref[...]
    for t in range(CHUNK):
        if t > 0:
            run = run + at(t) - at(t - win)
        if t + 1 >= win:
            inv = 1.0 / win
        else:
            inv = jnp.where(jnp.logical_and(first, row < nb), 1.0 / (t + 1), 1.0 / win)
        mixed = run * inv - cur(t)
        y = jnp.dot(mixed.astype(BF16), wmat, preferred_element_type=F32)
        o_ref[t] = h_ref[t] + y * scale


def _pool_kernel(u_ref, halo_ref, w_ref, sc_ref, h_ref, o_ref, *, nb, n_real):
    g = pl.program_id(0)
    first = pl.program_id(1) == n_real
    for gi, win in enumerate(POOL_WINDOWS):
        @pl.when(g == gi)
        def _(win=win):
            _pool_body(win, first, u_ref, halo_ref, w_ref, sc_ref, h_ref, o_ref, nb)


def _pool_core(u, w_grp, scale, h, nb, slp, nc):
    n, d = h.shape
    ng = len(POOL_WINDOWS)
    cw = d // ng
    rt = LANE
    per = rt // nb
    u3 = u.reshape(CHUNK, slp, d)
    h3 = h.reshape(CHUNK, slp, d)
    blk = pl.BlockSpec((CHUNK, rt, cw), lambda g, i: (0, i, g))

    def halo_map(g, i):
        return (0, jnp.where(i == 0, nc, jnp.maximum(i * per - 1, 0)), g)

    out = pl.pallas_call(
        functools.partial(_pool_kernel, nb=nb, n_real=nc // per),
        grid=(ng, slp // rt),
        in_specs=[blk,
                  pl.BlockSpec((CHUNK, nb, cw), halo_map),
                  pl.BlockSpec((None, cw, cw), lambda g, i: (g, 0, 0)),
                  pl.BlockSpec((1, cw), lambda g, i: (0, g)),
                  blk],
        out_specs=blk,
        out_shape=jax.ShapeDtypeStruct(h3.shape, F32),
        compiler_params=_cparams("parallel", "parallel"),
        name="pool_core",
    )(u3, u3, w_grp, scale.reshape(1, d), h3)
    return out.reshape(n, d)


def _router_kernel(h_ref, g_ref, whi_ref, wlo_ref, b_ref, eid_ref, wts_ref, rank_ref, cnt_ref, carry_s):
    i = pl.program_id(0)
    tm = h_ref.shape[0]

    @pl.when(i == 0)
    def _():
        carry_s[...] = jnp.zeros_like(carry_s)

    xn = _rms(h_ref[...], g_ref[...])
    x_hi = xn.astype(BF16)
    x_lo = (xn - x_hi.astype(F32)).astype(BF16)
    w_hi = whi_ref[...]
    lt = (jnp.dot(x_hi, w_hi, preferred_element_type=F32)
          + jnp.dot(x_lo, w_hi, preferred_element_type=F32)
          + jnp.dot(x_hi, wlo_ref[...], preferred_element_type=F32))
    logits = lt.T + b_ref[:, 0:1]

    c = [logits[r:r + 1] for r in range(N_EXPERT_GROUPS)]
    cmax = jnp.maximum(jnp.maximum(c[0], c[1]), jnp.maximum(c[2], c[3]))
    grp = jnp.where(c[0] == cmax, 0, jnp.where(c[1] == cmax, 1, jnp.where(c[2] == cmax, 2, 3)))
    csum = (jnp.exp(c[0] - cmax) + jnp.exp(c[1] - cmax)) + (jnp.exp(c[2] - cmax) + jnp.exp(c[3] - cmax))
    p_grp = 1.0 / csum

    epg = EXPERTS_PER_GROUP
    sel = jnp.zeros((epg, tm), F32)
    for gi in range(N_EXPERT_GROUPS):
        blk = logits[FINE_ROW0 + gi * epg:FINE_ROW0 + (gi + 1) * epg]
        sel = jnp.where(grp == gi, blk, sel)
    ridx = lax.broadcasted_iota(I32, (epg, tm), 0)
    m1 = jnp.max(sel, axis=0, keepdims=True)
    i1 = jnp.min(jnp.where(sel == m1, ridx, epg), axis=0, keepdims=True)
    sel2 = jnp.where(ridx == i1, -jnp.inf, sel)
    m2 = jnp.max(sel2, axis=0, keepdims=True)
    i2 = jnp.min(jnp.where(sel2 == m2, ridx, epg), axis=0, keepdims=True)
    ssum = jnp.sum(jnp.exp(sel - m1), axis=0, keepdims=True)
    p1 = 1.0 / ssum
    p2 = jnp.exp(m2 - m1) / ssum
    den = p1 + p2
    e1 = grp * epg + i1
    e2 = grp * epg + i2
    eid_ref[0:1, :] = e1
    eid_ref[1:2, :] = e2
    wts_ref[0:1, :] = p1 / den * p_grp
    wts_ref[1:2, :] = p2 / den * p_grp

    eidx = lax.broadcasted_iota(I32, (N_EXPERTS, tm), 0)
    hit1 = eidx == e1
    hit2 = eidx == e2
    onehot = jnp.where(hit1, 1.0, jnp.where(hit2, 1.0, 0.0))
    tri = jnp.where(lax.broadcasted_iota(I32, (tm, tm), 0) <= lax.broadcasted_iota(I32, (tm, tm), 1),
                    1.0, 0.0).astype(BF16)
    pref = jnp.dot(onehot.astype(BF16), tri, preferred_element_type=F32)
    tot = pref + carry_s[:, 0:1]
    rank_ref[0:1, :] = (jnp.sum(jnp.where(hit1, tot, 0.0), axis=0, keepdims=True) - 1.0).astype(I32)
    rank_ref[1:2, :] = (jnp.sum(jnp.where(hit2, tot, 0.0), axis=0, keepdims=True) - 1.0).astype(I32)
    carry_s[...] = carry_s[...] + pref[:, tm - 1:tm]
    cnt_ref[...] = carry_s[...]


def _router(h, g, w_hi, w_lo, bias):
    n, d = h.shape
    tm = TOK_TILE
    two = pl.BlockSpec((2, tm), lambda i: (0, i))
    return pl.pallas_call(
        _router_kernel,
        grid=(n // tm,),
        in_specs=[pl.BlockSpec((tm, d), lambda i: (i, 0)),
                  pl.BlockSpec((1, d), lambda i: (0, 0)),
                  pl.BlockSpec((d, ROUTER_ROWS), lambda i: (0, 0)),
                  pl.BlockSpec((d, ROUTER_ROWS), lambda i: (0, 0)),
                  pl.BlockSpec((ROUTER_ROWS, LANE), lambda i: (0, 0))],
        out_specs=[two, two, two, pl.BlockSpec((N_EXPERTS, LANE), lambda i: (0, 0))],
        out_shape=[jax.ShapeDtypeStruct((2, n), I32), jax.ShapeDtypeStruct((2, n), F32),
                   jax.ShapeDtypeStruct((2, n), I32), jax.ShapeDtypeStruct((N_EXPERTS, LANE), F32)],
        scratch_shapes=[pltpu.VMEM((N_EXPERTS, LANE), F32)],
        compiler_params=_cparams("arbitrary"),
        name="moe_router",
    )(h, g.reshape(1, d), w_hi, w_lo, bias)


def _router_params(w_coarse, b_coarse, w_fine, b_fine):
    d = w_coarse.shape[0]
    pad0 = jnp.zeros((d, FINE_ROW0 - N_EXPERT_GROUPS), F32)
    pad1 = jnp.zeros((d, ROUTER_ROWS - FINE_ROW0 - N_EXPERTS), F32)
    wr = jnp.concatenate([w_coarse.astype(F32), pad0, w_fine.astype(F32), pad1], axis=1)
    w_hi = wr.astype(BF16)
    w_lo = (wr - w_hi.astype(F32)).astype(BF16)
    bias = jnp.concatenate([b_coarse.astype(F32), pad0[0], b_fine.astype(F32), pad1[0]])
    return w_hi, w_lo, jnp.broadcast_to(bias[:, None], (ROUTER_ROWS, LANE))


def _route_tables(counts, n_tiles):
    cnt = counts[:, 0].astype(I32)
    tiles_e = (cnt + EXP_TILE - 1) // EXP_TILE
    tile_end = jnp.cumsum(tiles_e)
    tile_start = tile_end - tiles_e
    n_used = tile_end[-1]
    tq = jnp.minimum(jnp.arange(n_tiles, dtype=I32), n_used - 1)
    te = jnp.sum((tile_end[None, :] <= tq[:, None]).astype(I32), axis=1)
    onehot = (te[:, None] == jnp.arange(N_EXPERTS, dtype=I32)[None, :]).astype(I32)
    cnt_t = jnp.sum(onehot * cnt[None, :], axis=1)
    start_t = jnp.sum(onehot * tile_start[None, :], axis=1)
    tid = jnp.arange(n_tiles, dtype=I32)
    valid = jnp.where(tid < n_used, jnp.clip(cnt_t - (tid - start_t) * EXP_TILE, 0, EXP_TILE), 0)
    seg_start = jnp.broadcast_to((tile_start * EXP_TILE)[:, None], (N_EXPERTS, LANE))
    pad_start = tile_start * EXP_TILE + cnt
    ar = jnp.arange(N_EXPERTS, dtype=I32)
    live = tiles_e > 0
    order = jnp.cumsum(live.astype(I32)) - live.astype(I32)
    later = jnp.logical_and(ar[None, :] > ar[:, None], live[None, :])
    nxt_e = jnp.min(jnp.where(later, ar[None, :], N_EXPERTS), axis=1)
    nxt_e = jnp.where(nxt_e >= N_EXPERTS, -1, nxt_e)
    slot_t = jnp.sum(onehot * (order & 1)[None, :], axis=1)
    nxt_t = jnp.sum(onehot * nxt_e[None, :], axis=1)
    return (seg_start.astype(I32), pad_start.astype(I32), te.astype(I32), valid.astype(I32),
            jnp.reshape(n_used, (1,)).astype(I32), slot_t.astype(I32), nxt_t.astype(I32))


def _pos_kernel(eid_ref, rank_ref, st_ref, pos_ref):
    nsub, _, tm = pos_ref.shape
    eidx = lax.broadcasted_iota(I32, (N_EXPERTS, tm), 0)
    st = st_ref[:, 0:1]
    for j in range(nsub):
        sl = slice(j * tm, (j + 1) * tm)
        for k in range(2):
            base = jnp.sum(jnp.where(eidx == eid_ref[k:k + 1, sl], st, 0), axis=0, keepdims=True)
            pos_ref[j, k:k + 1, :] = base + rank_ref[k:k + 1, sl]


def _positions(eid, rank, seg_start):
    n = eid.shape[1]
    tm = TOK_TILE
    nsub = 8 if (n // tm) % 8 == 0 else 1
    two = pl.BlockSpec((2, nsub * tm), lambda i: (0, i))
    return pl.pallas_call(
        _pos_kernel,
        grid=(n // (nsub * tm),),
        in_specs=[two, two, pl.BlockSpec((N_EXPERTS, LANE), lambda i: (0, 0))],
        out_specs=pl.BlockSpec((nsub, 2, tm), lambda i: (i, 0, 0)),
        out_shape=jax.ShapeDtypeStruct((n // tm, 2, tm), I32),
        compiler_params=_cparams("parallel"),
        name="moe_positions",
    )(eid, rank, seg_start)


SUB = 8


def _dispatch_kernel(pad_ref, nu_ref, pos_ref, h_ref, g_ref, xs_ref, buf, zbuf, sem, zsem):
    i = pl.program_id(0)
    n = pl.num_programs(0)
    tm = h_ref.shape[0]
    half = h_ref.shape[1] // 2
    slot = i % 2

    def wait_slot(s):
        for _ in range(2):
            pltpu.make_async_copy(zbuf, xs_ref.at[pl.ds(0, tm)], sem.at[s]).wait()

    @pl.when(i == 0)
    def _():
        zbuf[...] = jnp.zeros_like(zbuf)
        zt = zbuf.shape[0]
        sizes = [1 << k for k in range(zt.bit_length() - 1)]

        def pad_pieces(e, act):
            off = pad_ref[e]
            length = (zt - (off & (zt - 1))) & (zt - 1)
            for sz in sizes:
                take = (length & sz) != 0

                @pl.when(take)
                def _(off=off, sz=sz):
                    if sz < SUB:
                        for r in range(sz):
                            act(pltpu.make_async_copy(zbuf.at[pl.ds(0, 1)], xs_ref.at[pl.ds(off + r, 1)], zsem))
                    else:
                        act(pltpu.make_async_copy(zbuf.at[pl.ds(0, sz)],
                                                  xs_ref.at[pl.ds(pl.multiple_of(off, SUB), sz)], zsem))

                off = off + jnp.where(take, sz, 0)

        def tail_piece(j, act):
            act(pltpu.make_async_copy(zbuf, xs_ref.at[pl.ds(pl.multiple_of(j * zt, zt), zt)], zsem))

        n_all = xs_ref.shape[0] // zt
        for act in (lambda c: c.start(), lambda c: c.wait()):
            lax.fori_loop(0, N_EXPERTS, lambda e, carry, act=act: (pad_pieces(e, act), carry)[1], 0)
            lax.fori_loop(nu_ref[0], n_all, lambda j, carry, act=act: (tail_piece(j, act), carry)[1], 0)

    @pl.when(i >= 2)
    def _():
        wait_slot(slot)

    xn = _rms(h_ref[...], g_ref[...])
    buf[slot] = _pack_pairs(xn[:, :half], xn[:, half:]).reshape(tm // SUB, SUB, half)

    def issue(q, carry):
        for s in range(SUB):
            for k in range(2):
                pltpu.make_async_copy(buf.at[slot, q, pl.ds(s, 1)],
                                      xs_ref.at[pl.ds(pos_ref[q, 2 * s + k], 1)],
                                      sem.at[slot]).start(priority=k)
        return carry

    lax.fori_loop(0, tm // SUB, issue, 0)

    @pl.when(i == n - 1)
    def _():
        wait_slot(slot)

        @pl.when(n >= 2)
        def _():
            wait_slot(1 - slot)


def _dispatch(h, g, pos3, pad_start, n_used, n_rows, word_dtype):
    n, d = h.shape
    tm = TOK_TILE
    assert tm == EXP_TILE
    half = d // 2
    grid_spec = pltpu.PrefetchScalarGridSpec(
        num_scalar_prefetch=2,
        grid=(n // tm,),
        in_specs=[pl.BlockSpec((None, tm // SUB, 2 * SUB), lambda i, pad, nu: (i, 0, 0), memory_space=pltpu.SMEM),
                  pl.BlockSpec((tm, d), lambda i, pad, nu: (i, 0)),
                  pl.BlockSpec((1, d), lambda i, pad, nu: (0, 0))],
        out_specs=pl.BlockSpec(memory_space=pl.ANY),
        scratch_shapes=[pltpu.VMEM((2, tm // SUB, SUB, half), word_dtype),
                        pltpu.VMEM((EXP_TILE, half), word_dtype),
                        pltpu.SemaphoreType.DMA((2,)), pltpu.SemaphoreType.DMA(())],
    )
    return pl.pallas_call(
        _dispatch_kernel,
        grid_spec=grid_spec,
        out_shape=jax.ShapeDtypeStruct((n_rows, half), word_dtype),
        compiler_params=_cparams("arbitrary"),
        name="moe_dispatch",
    )(pad_start, n_used, pos3, h, g.reshape(1, d))


def _expert_kernel(te_ref, va_ref, nu_ref, sl_ref, nx_ref, xs_ref, wg_hbm, wu_hbm, wd_hbm, ys_ref,
                   wgf, wuf, wdf, wgb, wub, wdb, wsem, *, layer):
    del nu_ref
    i = pl.program_id(0)
    e = te_ref[i]
    slot = sl_ref[i]
    changed = jnp.logical_or(i == 0, e != te_ref[jnp.maximum(i - 1, 0)])

    def weight_copies(ex, s):
        return [pltpu.make_async_copy(wg_hbm.at[layer, ex], wgf.at[s], wsem.at[s]),
                pltpu.make_async_copy(wu_hbm.at[layer, ex], wuf.at[s], wsem.at[s]),
                pltpu.make_async_copy(wd_hbm.at[layer, ex], wdf.at[s], wsem.at[s])]

    @pl.when(i == 0)
    def _():
        for c in weight_copies(e, slot):
            c.start()

    @pl.when(changed)
    def _():
        for c in weight_copies(e, slot):
            c.wait()
        wgb[...] = wgf[slot].astype(BF16)
        wub[...] = wuf[slot].astype(BF16)
        wdb[...] = wdf[slot].astype(BF16)
        nxt = nx_ref[i]

        @pl.when(nxt >= 0)
        def _():
            for c in weight_copies(nxt, 1 - slot):
                c.start()

    valid = va_ref[i]

    @pl.when(valid > 0)
    def _():
        lo, hi = _unpack_pairs(xs_ref[...])
        lo = lo.astype(BF16)
        hi = hi.astype(BF16)
        half = lo.shape[1]
        gate = (jnp.dot(lo, wgb[0:half, :], preferred_element_type=F32)
                + jnp.dot(hi, wgb[half:2 * half, :], preferred_element_type=F32))
        up = (jnp.dot(lo, wub[0:half, :], preferred_element_type=F32)
              + jnp.dot(hi, wub[half:2 * half, :], preferred_element_type=F32))
        hid = (gate * jax.nn.sigmoid(gate)) * up
        y = jnp.dot(hid.astype(BF16), wdb[...], preferred_element_type=F32)
        ys_ref[...] = _pack_pairs(y[:, :half], y[:, half:])

    @pl.when(valid <= 0)
    def _():
        ys_ref[...] = jnp.zeros_like(ys_ref)


def _experts(xs, te, valid, n_used, slot_t, nxt_t, w_gate, w_up, w_down, layer):
    n_tiles = te.shape[0]
    _, half = xs.shape
    _, _, d, de = w_gate.shape
    te_tile = EXP_TILE
    hbm = pl.BlockSpec(memory_space=pl.ANY)
    grid_spec = pltpu.PrefetchScalarGridSpec(
        num_scalar_prefetch=5,
        grid=(n_tiles,),
        in_specs=[pl.BlockSpec((te_tile, half), lambda i, te, va, nu, sl, nx: (jnp.minimum(i, nu[0] - 1), 0)),
                  hbm, hbm, hbm],
        out_specs=pl.BlockSpec((te_tile, half), lambda i, te, va, nu, sl, nx: (i, 0)),
        scratch_shapes=[pltpu.VMEM((2, d, de), F32), pltpu.VMEM((2, d, de), F32), pltpu.VMEM((2, de, d), F32),
                        pltpu.VMEM((d, de), BF16), pltpu.VMEM((d, de), BF16), pltpu.VMEM((de, d), BF16),
                        pltpu.SemaphoreType.DMA((2,))],
    )
    return pl.pallas_call(
        functools.partial(_expert_kernel, layer=layer),
        grid_spec=grid_spec,
        out_shape=jax.ShapeDtypeStruct((n_tiles * te_tile, half), xs.dtype),
        compiler_params=_cparams("arbitrary"),
        name="moe_experts",
    )(te, valid, n_used, slot_t, nxt_t, xs, w_gate, w_up, w_down)


def _combine_kernel(pos_ref, posn_ref, w_ref, h_ref, *rest, mode):
    if mode == "proj":
        g_ref, wn_ref, ys_ref, o_ref, nxt_ref, buf, sem = rest
    elif mode == "norm":
        g_ref, ys_ref, o_ref, nxt_ref, buf, sem = rest
    else:
        ys_ref, o_ref, buf, sem = rest
    i = pl.program_id(0)
    n = pl.num_programs(0)
    tm = h_ref.shape[0]
    half = h_ref.shape[1] // 2

    def row_copy(p_ref, s, q, sub, k):
        return pltpu.make_async_copy(ys_ref.at[pl.ds(p_ref[q, 2 * sub + k], 1)],
                                     buf.at[s, k, q, pl.ds(sub, 1)], sem.at[s])

    def wait_rows(s):
        for k in range(2):
            for _ in range(tm // SUB):
                pltpu.make_async_copy(ys_ref.at[pl.ds(0, SUB)], buf.at[s, k, 0], sem.at[s]).wait()

    @pl.when(i == 0)
    def _():
        def body(q, carry):
            for sub in range(SUB):
                for k in range(2):
                    row_copy(pos_ref, 0, q, sub, k).start(priority=k)
            return carry
        lax.fori_loop(0, tm // SUB, body, 0)

    def step(s):
        wait_rows(s)
        for q in range(tm // SUB):
            for sub in range(SUB):
                for k in range(2):
                    row_copy(posn_ref, 1 - s, q, sub, k).start(priority=k)
        lo0, hi0 = _unpack_pairs(buf[s, 0].reshape(tm, half))
        lo1, hi1 = _unpack_pairs(buf[s, 1].reshape(tm, half))
        w0 = jnp.broadcast_to(w_ref[0:1, :], (LANE, tm)).T[:, 0:1]
        w1 = jnp.broadcast_to(w_ref[1:2, :], (LANE, tm)).T[:, 0:1]
        h = h_ref[...]
        hnew = jnp.concatenate([h[:, :half] + (w0 * lo0 + w1 * lo1),
                                h[:, half:] + (w0 * hi0 + w1 * hi1)], axis=1)
        o_ref[...] = hnew
        if mode == "norm":
            nxt_ref[...] = _rms(hnew, g_ref[...]).astype(nxt_ref.dtype)
        elif mode == "proj":
            hn = _rms(hnew, g_ref[...]).astype(BF16)
            nxt_ref[...] = jnp.dot(hn, wn_ref[...], preferred_element_type=F32).astype(nxt_ref.dtype)

        @pl.when(i == n - 1)
        def _():
            wait_rows(1 - s)

    for s in range(2):
        pl.when(i % 2 == s)(functools.partial(step, s))


def _combine(h, ys, pos3, wts, g_next, w_next):
    n, d = h.shape
    tm = TOK_TILE
    nt = n // tm
    half = ys.shape[1]
    mode = "none" if g_next is None else ("norm" if w_next is None else "proj")
    smem = functools.partial(pl.BlockSpec, (None, tm // SUB, 2 * SUB), memory_space=pltpu.SMEM)
    row = pl.BlockSpec((tm, d), lambda i: (i, 0))
    in_specs = [smem(lambda i: (i, 0, 0)),
                smem(lambda i: (jnp.minimum(i + 1, nt - 1), 0, 0)),
                pl.BlockSpec((2, tm), lambda i: (0, i)),
                row]
    args = [pos3, pos3, wts, h]
    out_specs = [row]
    out_shape = [jax.ShapeDtypeStruct((n, d), F32)]
    if mode != "none":
        in_specs.append(pl.BlockSpec((1, d), lambda i: (0, 0)))
        args.append(g_next.reshape(1, d))
        if mode == "proj":
            f = w_next.shape[1]
            in_specs.append(pl.BlockSpec((d, f), lambda i: (0, 0)))
            args.append(w_next)
            out_specs.append(pl.BlockSpec((tm, f), lambda i: (i, 0)))
            out_shape.append(jax.ShapeDtypeStruct((n, f), F32))
        else:
            out_specs.append(row)
            out_shape.append(jax.ShapeDtypeStruct((n, d), BF16))
    in_specs.append(pl.BlockSpec(memory_space=pl.ANY))
    args.append(ys)
    out = pl.pallas_call(
        functools.partial(_combine_kernel, mode=mode),
        grid=(nt,),
        in_specs=in_specs,
        out_specs=out_specs,
        out_shape=out_shape,
        scratch_shapes=[pltpu.VMEM((2, 2, tm // SUB, SUB, half), ys.dtype), pltpu.SemaphoreType.DMA((2,))],
        compiler_params=_cparams("arbitrary"),
        name="moe_combine",
    )(*args)
    return (out[0], out[1]) if mode != "none" else (out[0], None)


def _hier_moe(h, g_ffn, g_next, w_next, layer, w_coarse, b_coarse, w_fine, b_fine, w_gate, w_up, w_down):
    n = h.shape[0]
    n_tiles = (2 * n + N_EXPERTS * (EXP_TILE - 1) + EXP_TILE - 1) // EXP_TILE
    w_hi, w_lo, bias = _router_params(w_coarse, b_coarse, w_fine, b_fine)
    eid, wts, rank, counts = _router(h, g_ffn, w_hi, w_lo, bias)
    seg_start, pad_start, te, valid, n_used, slot_t, nxt_t = _route_tables(counts, n_tiles)
    pos3 = _positions(eid, rank, seg_start)
    nt = pos3.shape[0]
    pos3 = pos3.reshape(nt, 2, TOK_TILE // SUB, SUB).transpose(0, 2, 3, 1).reshape(nt, TOK_TILE // SUB, 2 * SUB)
    word_dtype = jax.eval_shape(lambda a: _pack_pairs(a, a), jax.ShapeDtypeStruct((8, LANE), F32)).dtype
    xs = _dispatch(h, g_ffn, pos3, pad_start, n_used, n_tiles * EXP_TILE, word_dtype)
    ys = _experts(xs, te, valid, n_used, slot_t, nxt_t, w_gate, w_up, w_down, layer)
    return _combine(h, ys, pos3, wts, g_next, w_next)


def kernel(x, meta_tokens, norm_mix, norm_ffn, norm_final, s5_w_in, s5_a_re, s5_a_im, s5_log_dt,
           s5_b_re, s5_b_im, s5_c_re, s5_c_im, s5_d, s5_w_glu, pool_w_in, pool_w_grp, pool_scale,
           moe_w_coarse, moe_b_coarse, moe_w_fine, moe_b_fine, moe_w_gate, moe_w_up, moe_w_down):
    b, seq, d = x.shape
    depth = norm_mix.shape[0]
    assert meta_tokens.shape[0] == N_META == CHUNK and seq % CHUNK == 0
    assert LANE % b == 0 and b % 16 == 0
    per = LANE // b
    nc = seq // CHUNK
    assert nc % per == 0
    ncp = nc + per
    slp = ncp * b

    h, hn = _ingest(x, meta_tokens, norm_mix[0], ncp)
    for i in range(depth):
        j = i // 2
        if i % 2 == 0:
            g = s5_a_re.shape[1]
            ops = _s5_prep(s5_a_re[j], s5_a_im[j], s5_log_dt[j], s5_b_re[j], s5_b_im[j],
                           s5_c_re[j], s5_c_im[j])
            d_b = jnp.broadcast_to(jnp.tile(s5_d[j].reshape(g, 1, S5_GROUP), (1, CHUNK, 1))
                                   .reshape(g, CHUNK_W, 1), (g, CHUNK_W, LANE))
            ut = _s5_in(hn, s5_w_in[j].astype(BF16), slp)
            zt = _s5_core(ut, ops, d_b, b, nc)
            h = _s5_out(zt, s5_w_glu[j].astype(BF16), h, slp)
        else:
            h = _pool_core(hn, pool_w_grp[j].astype(BF16), pool_scale[j], h, b, slp, nc)
        g_next = None if i == depth - 1 else norm_mix[i + 1]
        w_next = pool_w_in[(i + 1) // 2].astype(BF16) if (i + 1 < depth and (i + 1) % 2 == 1) else None
        h, hn = _hier_moe(h, norm_ffn[i], g_next, w_next, i,
                          moe_w_coarse[i], moe_b_coarse[i], moe_w_fine[i], moe_b_fine[i],
                          moe_w_gate, moe_w_up, moe_w_down)
    return _egress(h, norm_final, b, seq, ncp)
```
